```python
import math
import jax, jax.numpy as jnp
from jax import lax
import numpy as np

D_MODEL = 2048
BATCH = 16
SEQ = 2048
DEPTH = 1
DEC_BATCH = 2
DEC_SEQ = 16384
PAST_LEN = 128

GRID_W = 64
Q_BLOCK = 128
RMS_EPS = 1e-6
N_HEADS_A = 8
N_KV_A = 2
HEAD_DIM_A = 128
AXIAL_THETA = 10000.0
N_HEADS_B = 4
HEAD_DIM_B = 128
V_DIM_B = 2 * HEAD_DIM_B
ROPE_THETA = 500000.0
ROT_DIM_B = HEAD_DIM_B // 4
MEM_LEN = 256
N_HEADS_X = 4
HEAD_DIM_X = 128
N_EXPERTS = 32
TOP_K = 4
D_FF = D_MODEL
SWIGLU_ALPHA = 1.702
SWIGLU_LIMIT = 7.0

WIDTH_A = N_HEADS_A * HEAD_DIM_A
WIDTH_B = N_HEADS_B * V_DIM_B
WIDTH_X = N_HEADS_X * HEAD_DIM_X
IN_SPLITS = (N_HEADS_A * HEAD_DIM_A, N_KV_A * HEAD_DIM_A, N_KV_A * HEAD_DIM_A,
             2 * N_HEADS_B * HEAD_DIM_B, 2 * N_HEADS_B * HEAD_DIM_B, N_HEADS_B * V_DIM_B,
             D_MODEL, D_MODEL)
IN_COLS = sum(IN_SPLITS)

kernel_name = 'hybrid_gqa_diffattn_memxattn_moe_encoder'


def rms_norm(x, g):
    xf = x.astype(jnp.float32)
    y = xf * lax.rsqrt(jnp.mean(xf * xf, axis=-1, keepdims=True) + RMS_EPS)
    return (y * g.astype(jnp.float32)).astype(x.dtype)


def rope(x, pos, theta):
    d = x.shape[-1]
    half = d // 2
    inv = jnp.power(theta, -2.0 * jnp.arange(half, dtype=jnp.float32) / d)
    ang = pos[:, None] * inv[None, :]
    cos = jnp.cos(ang)[:, None, :]
    sin = jnp.sin(ang)[:, None, :]
    xf = x.astype(jnp.float32)
    x1, x2 = xf[..., :half], xf[..., half:]
    return jnp.concatenate([x1 * cos - x2 * sin, x2 * cos + x1 * sin], axis=-1).astype(x.dtype)


def axial_rope(x, row, col):
    h = x.shape[-1] // 2
    return jnp.concatenate([rope(x[..., :h], row, AXIAL_THETA), rope(x[..., h:], col, AXIAL_THETA)], axis=-1)


def partial_rope(x, pos):
    return jnp.concatenate([rope(x[..., :ROT_DIM_B], pos, ROPE_THETA), x[..., ROT_DIM_B:]], axis=-1)


def grid_positions(n_tok):
    rows = n_tok // GRID_W
    r = jnp.repeat(jnp.arange(rows, dtype=jnp.float32), GRID_W)
    c = jnp.tile(jnp.arange(GRID_W, dtype=jnp.float32), rows)
    return r, c


def split_cols(p):
    offsets = [int(o) for o in np.cumsum(IN_SPLITS)[:-1]]
    return jnp.split(p, offsets, axis=-1)


def gqa_attention_blocks(q, k, v):
    B, S, Hq, d = q.shape
    Hkv = k.shape[2]
    G = Hq // Hkv
    nb = S // Q_BLOCK
    qb = q.reshape(B, nb, Q_BLOCK, Hkv, G, d).transpose(1, 0, 2, 3, 4, 5)
    scale = d ** -0.5

    def one_block(qblk):
        s = jnp.einsum('bqhgd,bkhd->bhgqk', qblk, k).astype(jnp.float32) * scale
        p = jax.nn.softmax(s, axis=-1).astype(v.dtype)
        return jnp.einsum('bhgqk,bkhd->bqhgd', p, v)

    o = lax.map(one_block, qb)
    return o.transpose(1, 0, 2, 3, 4, 5).reshape(B, S, Hq, d)


def diff_attention_blocks(q, k, v, lam):
    B, S, _, d = q.shape
    H = v.shape[2]
    nb = S // Q_BLOCK
    k = k.reshape(B, S, H, 2, d)
    qb = q.reshape(B, nb, Q_BLOCK, H, 2, d).transpose(1, 0, 2, 3, 4, 5)
    scale = d ** -0.5

    def one_block(qblk):
        s = jnp.einsum('bqhcd,bkhcd->bhcqk', qblk, k).astype(jnp.float32) * scale
        p = jax.nn.softmax(s, axis=-1)
        pd = p[:, :, 0] - lam * p[:, :, 1]
        return jnp.einsum('bhqk,bkhe->bqhe', pd.astype(v.dtype), v)

    o = lax.map(one_block, qb)
    return o.transpose(1, 0, 2, 3, 4).reshape(B, S, H, v.shape[-1])


def parallel_mix(xn, layer_idx, w_in, q_norm_a, k_norm_a, lambda_q1, lambda_k1, lambda_q2, lambda_k2,
                 subln_b, w_branch_a, w_branch_b, w_out):
    B, S, _ = xn.shape
    qa, ka, va, qb, kb, vb, ga, gb = split_cols(xn @ w_in)
    row, col = grid_positions(S)
    qa = axial_rope(rms_norm(qa.reshape(B, S, N_HEADS_A, HEAD_DIM_A), q_norm_a), row, col)
    ka = axial_rope(rms_norm(ka.reshape(B, S, N_KV_A, HEAD_DIM_A), k_norm_a), row, col)
    va = va.reshape(B, S, N_KV_A, HEAD_DIM_A)
    oa = gqa_attention_blocks(qa, ka, va).reshape(B, S, WIDTH_A)
    pos = jnp.arange(S, dtype=jnp.float32)
    qb = partial_rope(qb.reshape(B, S, 2 * N_HEADS_B, HEAD_DIM_B), pos)
    kb = partial_rope(kb.reshape(B, S, 2 * N_HEADS_B, HEAD_DIM_B), pos)
    vb = vb.reshape(B, S, N_HEADS_B, V_DIM_B)
    lam_init = 0.8 - 0.6 * math.exp(-0.3 * layer_idx)
    lam = (jnp.exp(jnp.sum(lambda_q1.astype(jnp.float32) * lambda_k1.astype(jnp.float32)))
           - jnp.exp(jnp.sum(lambda_q2.astype(jnp.float32) * lambda_k2.astype(jnp.float32))) + lam_init)
    ob = diff_attention_blocks(qb, kb, vb, lam)
    ob = (rms_norm(ob, subln_b) * (1.0 - lam_init)).reshape(B, S, WIDTH_B)
    merged = jax.nn.sigmoid(ga) * (oa @ w_branch_a) + jax.nn.sigmoid(gb) * (ob @ w_branch_b)
    return merged @ w_out


def memory_cross_attention(hn, memn, w_cq, w_ckv, w_co):
    B, S, _ = hn.shape
    M = memn.shape[1]
    q = (hn @ w_cq).reshape(B, S, N_HEADS_X, HEAD_DIM_X)
    kv = memn @ w_ckv
    k = kv[..., :WIDTH_X].reshape(B, M, N_HEADS_X, HEAD_DIM_X)
    v = kv[..., WIDTH_X:].reshape(B, M, N_HEADS_X, HEAD_DIM_X)
    s = jnp.einsum('bqhd,bkhd->bhqk', q, k).astype(jnp.float32) * (HEAD_DIM_X ** -0.5)
    p = jax.nn.softmax(s, axis=-1).astype(v.dtype)
    o = jnp.einsum('bhqk,bkhd->bqhd', p, v).reshape(B, S, WIDTH_X)
    return o @ w_co


def clamped_swiglu(h):
    glu, lin = h[..., ::2], h[..., 1::2]
    glu = jnp.minimum(glu, SWIGLU_LIMIT)
    lin = jnp.clip(lin, -SWIGLU_LIMIT, SWIGLU_LIMIT)
    return glu * jax.nn.sigmoid(SWIGLU_ALPHA * glu) * (lin + 1.0)


def moe(hn, w_router, b_router, w_exp1, b_exp1, w_exp2, b_exp2):
    B, S, D = hn.shape
    xt = hn.reshape(B * S, D)
    logits = (xt @ w_router + b_router).astype(jnp.float32)
    top_v, top_i = lax.top_k(logits, TOP_K)
    top_w = jax.nn.softmax(top_v, axis=-1)
    gate = jnp.sum(jax.nn.one_hot(top_i, N_EXPERTS, dtype=jnp.float32) * top_w[..., None], axis=1)
    out = jnp.zeros((B * S, D), jnp.float32)
    for e in range(N_EXPERTS):
        y = clamped_swiglu(xt @ w_exp1[e] + b_exp1[e]) @ w_exp2[e] + b_exp2[e]
        out = out + gate[:, e:e + 1] * y
    return out.astype(hn.dtype).reshape(B, S, D)


def encode(x, mem, norm_mix, w_in, q_norm_a, k_norm_a, lambda_q1, lambda_k1, lambda_q2, lambda_k2,
           subln_b, w_branch_a, w_branch_b, w_out, norm_cross, norm_mem, w_cq, w_ckv, w_co,
           norm_moe, w_router, b_router, w_exp1, b_exp1, w_exp2, b_exp2, norm_final):
    for l in range(DEPTH):
        x = x + parallel_mix(rms_norm(x, norm_mix[l]), l, w_in[l], q_norm_a[l], k_norm_a[l],
                             lambda_q1[l], lambda_k1[l], lambda_q2[l], lambda_k2[l], subln_b[l],
                             w_branch_a[l], w_branch_b[l], w_out[l])
        x = x + memory_cross_attention(rms_norm(x, norm_cross[l]), rms_norm(mem, norm_mem[l]),
                                       w_cq[l], w_ckv[l], w_co[l])
        x = x + moe(rms_norm(x, norm_moe[l]), w_router[l], b_router[l],
                    w_exp1[l], b_exp1[l], w_exp2[l], b_exp2[l])
    return rms_norm(x, norm_final)


def setup_inputs(seed: int = 0) -> dict:
    key = jax.random.key(seed)
    ks = jax.random.split(key, 29)
    f32 = jnp.float32
    L, D = DEPTH, D_MODEL

    def normal(k, shape, scale):
        return scale * jax.random.normal(k, shape, f32)

    def gain(k, shape):
        return 1.0 + 0.01 * jax.random.normal(k, shape, f32)

    return {
        'x_prompt': normal(ks[0], (BATCH, SEQ, D), 1.0),
        'x_sample': normal(ks[1], (DEC_BATCH, DEC_SEQ, D), 1.0),
        'mem_prompt': normal(ks[2], (BATCH, MEM_LEN, D), 1.0),
        'mem_sample': normal(ks[3], (DEC_BATCH, MEM_LEN, D), 1.0),
        'norm_mix': gain(ks[4], (L, D)),
        'w_in': normal(ks[5], (L, D, IN_COLS), D ** -0.5),
        'q_norm_a': gain(ks[6], (L, HEAD_DIM_A)),
        'k_norm_a': gain(ks[7], (L, HEAD_DIM_A)),
        'lambda_q1': normal(ks[8], (L, HEAD_DIM_B), 0.1),
        'lambda_k1': normal(ks[9], (L, HEAD_DIM_B), 0.1),
        'lambda_q2': normal(ks[10], (L, HEAD_DIM_B), 0.1),
        'lambda_k2': normal(ks[11], (L, HEAD_DIM_B), 0.1),
        'subln_b': gain(ks[12], (L, V_DIM_B)),
        'w_branch_a': normal(ks[13], (L, WIDTH_A, D), WIDTH_A ** -0.5),
        'w_branch_b': normal(ks[14], (L, WIDTH_B, D), WIDTH_B ** -0.5),
        'w_out': normal(ks[15], (L, D, D), D ** -0.5),
        'norm_cross': gain(ks[16], (L, D)),
        'norm_mem': gain(ks[17], (L, D)),
        'w_cq': normal(ks[18], (L, D, WIDTH_X), D ** -0.5),
        'w_ckv': normal(ks[19], (L, D, 2 * WIDTH_X), D ** -0.5),
        'w_co': normal(ks[20], (L, WIDTH_X, D), WIDTH_X ** -0.5),
        'norm_moe': gain(ks[21], (L, D)),
        'w_router': normal(ks[22], (L, D, N_EXPERTS), D ** -0.5),
        'b_router': normal(ks[23], (L, N_EXPERTS), 0.01),
        'w_exp1': normal(ks[24], (L, N_EXPERTS, D, 2 * D_FF), D ** -0.5),
        'b_exp1': normal(ks[25], (L, N_EXPERTS, 2 * D_FF), 0.01),
        'w_exp2': normal(ks[26], (L, N_EXPERTS, D_FF, D), D_FF ** -0.5),
        'b_exp2': normal(ks[27], (L, N_EXPERTS, D), 0.01),
        'norm_final': gain(ks[28], (D,)),
    }


def reference(x_prompt, x_sample, mem_prompt, mem_sample, norm_mix, w_in, q_norm_a, k_norm_a,
              lambda_q1, lambda_k1, lambda_q2, lambda_k2, subln_b, w_branch_a, w_branch_b, w_out,
              norm_cross, norm_mem, w_cq, w_ckv, w_co, norm_moe, w_router, b_router,
              w_exp1, b_exp1, w_exp2, b_exp2, norm_final):
    y_prompt = encode(x_prompt, mem_prompt, norm_mix, w_in, q_norm_a, k_norm_a, lambda_q1, lambda_k1,
                      lambda_q2, lambda_k2, subln_b, w_branch_a, w_branch_b, w_out, norm_cross, norm_mem,
                      w_cq, w_ckv, w_co, norm_moe, w_router, b_router, w_exp1, b_exp1, w_exp2, b_exp2,
                      norm_final)
    y_sample = encode(x_sample, mem_sample, norm_mix, w_in, q_norm_a, k_norm_a, lambda_q1, lambda_k1,
                      lambda_q2, lambda_k2, subln_b, w_branch_a, w_branch_b, w_out, norm_cross, norm_mem,
                      w_cq, w_ckv, w_co, norm_moe, w_router, b_router, w_exp1, b_exp1, w_exp2, b_exp2,
                      norm_final)
    return (y_prompt, y_sample)
```

```python
import functools
import math

import jax
import jax.numpy as jnp
from jax import lax
from jax.experimental import pallas as pl
from jax.experimental.pallas import tpu as pltpu

F32 = jnp.float32
BF16 = jnp.bfloat16

D_MODEL = 2048
DEPTH = 1
GRID_W = 64
RMS_EPS = 1e-6
N_HEADS_A = 8
N_KV_A = 2
HEAD_DIM = 128
AXIAL_THETA = 10000.0
N_HEADS_B = 4
ROPE_THETA = 500000.0
ROT_DIM_B = HEAD_DIM // 4
MEM_LEN = 256
N_HEADS_X = 4
WIDTH_X = N_HEADS_X * HEAD_DIM
N_EXPERTS = 32
TOP_K = 4
D_FF = D_MODEL
SWIGLU_ALPHA = 1.702
SWIGLU_LIMIT = 7.0
LANES = 128
IN_COLS = 8704
QK_SCALE = HEAD_DIM ** -0.5
NEG_BIG = -1e30

VMEM_LIMIT = 56 * 1024 * 1024

COL_GA = 0
COL_GB = 2048
COL_QA = 4096
COL_KA = 5120
COL_VA = 5376
COL_QB = 5632
COL_KB = 6656
COL_VB = 7680
REF_GATE_START = 4608


def _cparams(sem):
    return pltpu.CompilerParams(dimension_semantics=sem, vmem_limit_bytes=VMEM_LIMIT)


def _rms(x, g):
    return x * lax.rsqrt(jnp.mean(x * x, axis=-1, keepdims=True) + RMS_EPS) * g


def _rotate(x, cos, sin_signed, shift):
    n = x.shape[-1]
    lane = lax.broadcasted_iota(jnp.int32, x.shape, 1)
    up = pltpu.roll(x, n - shift, 1)
    dn = pltpu.roll(x, shift, 1)
    partner = jnp.where((lane % (2 * shift)) < shift, up, dn)
    return x * cos + partner * sin_signed


PROJ_TN = 512


def _proj_kernel(x_ref, g_ref, w_ref, ca_ref, sa_ref, cb_ref, sb_ref, qk_ref, o_ref, xn_ref):
    j = pl.program_id(1)

    @pl.when(j == 0)
    def _():
        xn_ref[...] = _rms(x_ref[...], g_ref[...]).astype(BF16)

    acc = jnp.dot(xn_ref[...], w_ref[...], preferred_element_type=F32)
    chunks = [acc[:, c * LANES:(c + 1) * LANES] for c in range(PROJ_TN // LANES)]

    def axial(a, gain, scale):
        r = _rotate(_rms(a, gain), ca_ref[...], sa_ref[...], HEAD_DIM // 4)
        return r * scale if scale != 1.0 else r

    def partial_rot(a, scale):
        r = _rotate(a, cb_ref[...], sb_ref[...], ROT_DIM_B // 2)
        return r * scale if scale != 1.0 else r

    def store(vals):
        for c, v in enumerate(vals):
            o_ref[:, c * LANES:(c + 1) * LANES] = v.astype(o_ref.dtype)

    @pl.when(j < 8)
    def _():
        store([jax.nn.sigmoid(a) for a in chunks])

    @pl.when((j >= 8) & (j < 10))
    def _():
        store([axial(a, qk_ref[0:1, :], QK_SCALE) for a in chunks])

    @pl.when(j == 10)
    def _():
        store([axial(chunks[0], qk_ref[1:2, :], 1.0), axial(chunks[1], qk_ref[1:2, :], 1.0), chunks[2], chunks[3]])

    @pl.when((j >= 11) & (j < 13))
    def _():
        store([partial_rot(a, QK_SCALE) for a in chunks])

    @pl.when((j >= 13) & (j < 15))
    def _():
        store([partial_rot(a, 1.0) for a in chunks])

    @pl.when(j >= 15)
    def _():
        store(chunks)


def _rope_tables(seq):
    t = jnp.arange(seq, dtype=jnp.int32)
    row = (t // GRID_W).astype(F32)
    col = (t % GRID_W).astype(F32)
    lane = jnp.arange(LANES, dtype=jnp.int32)
    half = HEAD_DIM // 4
    inv_a = jnp.power(AXIAL_THETA, -2.0 * (lane % half).astype(F32) / (HEAD_DIM // 2))
    pos_a = jnp.where(lane[None, :] < HEAD_DIM // 2, row[:, None], col[:, None])
    ang_a = pos_a * inv_a[None, :]
    sign_a = jnp.where((lane % (2 * half)) < half, -1.0, 1.0).astype(F32)
    ca, sa = jnp.cos(ang_a), jnp.sin(ang_a) * sign_a[None, :]
    hb = ROT_DIM_B // 2
    inv_b = jnp.power(ROPE_THETA, -2.0 * (lane % hb).astype(F32) / ROT_DIM_B)
    ang_b = t.astype(F32)[:, None] * inv_b[None, :]
    rot = (lane < ROT_DIM_B)[None, :]
    sign_b = jnp.where((lane % (2 * hb)) < hb, -1.0, 1.0).astype(F32)
    cb = jnp.where(rot, jnp.cos(ang_b), 1.0)
    sb = jnp.where(rot, jnp.sin(ang_b) * sign_b[None, :], 0.0)
    return ca, sa, cb, sb


def _proj(x, g, w, tables, qk_gain, seq, tm=1024):
    T = x.shape[0]
    tm = min(tm, seq)
    nseq = seq // tm
    tab_spec = pl.BlockSpec((tm, LANES), lambda i, j: (i % nseq, 0))
    return pl.pallas_call(
        _proj_kernel,
        grid=(T // tm, IN_COLS // PROJ_TN),
        in_specs=[
            pl.BlockSpec((tm, D_MODEL), lambda i, j: (i, 0)),
            pl.BlockSpec((1, D_MODEL), lambda i, j: (0, 0)),
            pl.BlockSpec((D_MODEL, PROJ_TN), lambda i, j: (0, j)),
            tab_spec, tab_spec, tab_spec, tab_spec,
            pl.BlockSpec((2, LANES), lambda i, j: (0, 0)),
        ],
        out_specs=pl.BlockSpec((tm, PROJ_TN), lambda i, j: (i, j)),
        out_shape=jax.ShapeDtypeStruct((T, IN_COLS), BF16),
        scratch_shapes=[pltpu.VMEM((tm, D_MODEL), BF16)],
        compiler_params=_cparams(("parallel", "arbitrary")),
        name="proj",
    )(x, g, w, *tables, qk_gain)


def _online_softmax_step(s, v, m_ref, l_ref, acc_ref):
    m_prev = m_ref[...]
    m_new = jnp.maximum(m_prev, jnp.max(s, axis=-1, keepdims=True))
    alpha = jnp.exp(m_prev - m_new)
    p = jnp.exp(s - m_new)
    l_ref[...] = alpha * l_ref[...] + jnp.sum(p, axis=-1, keepdims=True)
    acc_ref[...] = alpha * acc_ref[...] + jnp.dot(p.astype(BF16), v, preferred_element_type=F32)
    m_ref[...] = m_new


def _nt_dot(a, b):
    return lax.dot_general(a, b, (((1,), (1,)), ((), ())), preferred_element_type=F32)


def _attn_a_kernel(q_ref, k_ref, v_ref, o_ref, m_ref, l_ref, acc_ref, *, tk):
    tq = q_ref.shape[0]
    group = N_HEADS_A // N_KV_A
    q = jnp.concatenate([q_ref[:, g * HEAD_DIM:(g + 1) * HEAD_DIM] for g in range(group)], axis=0)
    m_ref[...] = jnp.full(m_ref.shape, NEG_BIG, F32)
    l_ref[...] = jnp.zeros(l_ref.shape, F32)
    acc_ref[...] = jnp.zeros(acc_ref.shape, F32)

    def body(i, carry):
        off = pl.multiple_of(i * tk, tk)
        s = _nt_dot(q, k_ref[pl.ds(off, tk), :])
        _online_softmax_step(s, v_ref[pl.ds(off, tk), :], m_ref, l_ref, acc_ref)
        return carry

    lax.fori_loop(0, k_ref.shape[0] // tk, body, 0)
    o = acc_ref[...] / l_ref[...]
    for g in range(group):
        o_ref[:, g * HEAD_DIM:(g + 1) * HEAD_DIM] = o[g * tq:(g + 1) * tq].astype(o_ref.dtype)


def _attn_a(p, batch, seq, tq=256, tk=512):
    T = p.shape[0]
    group = N_HEADS_A // N_KV_A
    qw = group * HEAD_DIM
    nq = seq // tq
    tk = min(tk, seq)
    return pl.pallas_call(
        functools.partial(_attn_a_kernel, tk=tk),
        grid=(batch, N_KV_A, nq),
        in_specs=[
            pl.BlockSpec((tq, qw), lambda b, h, i: (b * nq + i, COL_QA // qw + h)),
            pl.BlockSpec((seq, HEAD_DIM), lambda b, h, i: (b, COL_KA // HEAD_DIM + h)),
            pl.BlockSpec((seq, HEAD_DIM), lambda b, h, i: (b, COL_VA // HEAD_DIM + h)),
        ],
        out_specs=pl.BlockSpec((tq, qw), lambda b, h, i: (b * nq + i, h)),
        out_shape=jax.ShapeDtypeStruct((T, N_HEADS_A * HEAD_DIM), BF16),
        scratch_shapes=[pltpu.VMEM((group * tq, 1), F32), pltpu.VMEM((group * tq, 1), F32),
                        pltpu.VMEM((group * tq, HEAD_DIM), F32)],
        compiler_params=_cparams(("parallel", "parallel", "parallel")),
        name="attn_a",
    )(p, p, p)


def _attn_b_kernel(lam_ref, q_ref, k_ref, v_ref, sub_ref, o_ref, m_ref, l_ref, acc_ref, *, tk, lam_init):
    tq = q_ref.shape[0]
    q1 = q_ref[:, :HEAD_DIM]
    q2 = q_ref[:, HEAD_DIM:]
    m_ref[...] = jnp.full(m_ref.shape, NEG_BIG, F32)
    l_ref[...] = jnp.zeros(l_ref.shape, F32)
    acc_ref[...] = jnp.zeros(acc_ref.shape, F32)

    def body(i, carry):
        off = pl.multiple_of(i * tk, tk)
        s = jnp.concatenate([_nt_dot(q1, k_ref[pl.ds(off, tk), :HEAD_DIM]),
                             _nt_dot(q2, k_ref[pl.ds(off, tk), HEAD_DIM:])], axis=0)
        _online_softmax_step(s, v_ref[pl.ds(off, tk), :], m_ref, l_ref, acc_ref)
        return carry

    lax.fori_loop(0, k_ref.shape[0] // tk, body, 0)
    lv = lam_ref[...]
    lam = (jnp.exp(jnp.sum(lv[0:1] * lv[1:2], axis=-1, keepdims=True))
           - jnp.exp(jnp.sum(lv[2:3] * lv[3:4], axis=-1, keepdims=True)) + lam_init)
    o = acc_ref[...] / l_ref[...]
    od = o[:tq] - lam * o[tq:]
    o_ref[...] = (_rms(od, sub_ref[...]) * (1.0 - lam_init)).astype(o_ref.dtype)


def _attn_b(p, lam_vecs, subln, batch, seq, lam_init, tq=512, tk=512):
    T = p.shape[0]
    vw = 2 * HEAD_DIM
    tq = min(tq, seq)
    tk = min(tk, seq)
    nq = seq // tq
    return pl.pallas_call(
        functools.partial(_attn_b_kernel, tk=tk, lam_init=lam_init),
        grid=(batch, N_HEADS_B, nq),
        in_specs=[
            pl.BlockSpec((4, HEAD_DIM), lambda b, h, i: (0, 0)),
            pl.BlockSpec((tq, vw), lambda b, h, i: (b * nq + i, COL_QB // vw + h)),
            pl.BlockSpec((seq, vw), lambda b, h, i: (b, COL_KB // vw + h)),
            pl.BlockSpec((seq, vw), lambda b, h, i: (b, COL_VB // vw + h)),
            pl.BlockSpec((1, vw), lambda b, h, i: (0, 0)),
        ],
        out_specs=pl.BlockSpec((tq, vw), lambda b, h, i: (b * nq + i, h)),
        out_shape=jax.ShapeDtypeStruct((T, N_HEADS_B * vw), BF16),
        scratch_shapes=[pltpu.VMEM((2 * tq, 1), F32), pltpu.VMEM((2 * tq, 1), F32),
                        pltpu.VMEM((2 * tq, vw), F32)],
        compiler_params=_cparams(("parallel", "parallel", "parallel")),
        name="attn_b",
    )(lam_vecs, p, p, p, subln)


def _merge_kernel(x_ref, oa_ref, ob_ref, ga_ref, gb_ref, wa_ref, wb_ref, wo_ref, o_ref):
    a = jnp.dot(oa_ref[...], wa_ref[...], preferred_element_type=F32)
    b = jnp.dot(ob_ref[...], wb_ref[...], preferred_element_type=F32)
    merged = ga_ref[...].astype(F32) * a + gb_ref[...].astype(F32) * b
    o_ref[...] = x_ref[...] + jnp.dot(merged.astype(BF16), wo_ref[...], preferred_element_type=F32)


def _resident(shape):
    return pl.BlockSpec(shape, lambda *_: (0,) * len(shape), pipeline_mode=pl.Buffered(1))


def _merge(x, oa, ob, p, wa, wb, wo, tm=256):
    T = x.shape[0]
    return pl.pallas_call(
        _merge_kernel,
        grid=(T // tm,),
        in_specs=[
            pl.BlockSpec((tm, D_MODEL), lambda i: (i, 0)),
            pl.BlockSpec((tm, oa.shape[1]), lambda i: (i, 0)),
            pl.BlockSpec((tm, ob.shape[1]), lambda i: (i, 0)),
            pl.BlockSpec((tm, D_MODEL), lambda i: (i, COL_GA // D_MODEL)),
            pl.BlockSpec((tm, D_MODEL), lambda i: (i, COL_GB // D_MODEL)),
            _resident(wa.shape), _resident(wb.shape), _resident(wo.shape),
        ],
        out_specs=pl.BlockSpec((tm, D_MODEL), lambda i: (i, 0)),
        out_shape=jax.ShapeDtypeStruct((T, D_MODEL), F32),
        compiler_params=_cparams(("parallel",)),
        name="merge",
    )(x, oa, ob, p, p, wa, wb, wo)


def _norm_matmul_kernel(x_ref, g_ref, w_ref, o_ref):
    xn = _rms(x_ref[...], g_ref[...]).astype(BF16)
    o_ref[...] = jnp.dot(xn, w_ref[...], preferred_element_type=F32).astype(o_ref.dtype)


def _norm_matmul(x, g, w, tm=256):
    T, K = x.shape
    N = w.shape[1]
    return pl.pallas_call(
        _norm_matmul_kernel,
        grid=(T // tm,),
        in_specs=[pl.BlockSpec((tm, K), lambda i: (i, 0)), pl.BlockSpec((1, K), lambda i: (0, 0)),
                  pl.BlockSpec((K, N), lambda i: (0, 0))],
        out_specs=pl.BlockSpec((tm, N), lambda i: (i, 0)),
        out_shape=jax.ShapeDtypeStruct((T, N), BF16),
        compiler_params=_cparams(("parallel",)),
        name="mem_kv",
    )(x, g, w)


def _cross_kernel(x_ref, g_ref, wq_ref, kv_ref, wo_ref, o_ref):
    x = x_ref[...]
    hn = _rms(x, g_ref[...]).astype(BF16)
    q = (jnp.dot(hn, wq_ref[...], preferred_element_type=F32) * QK_SCALE).astype(BF16)
    heads = []
    for h in range(N_HEADS_X):
        lo = h * HEAD_DIM
        s = _nt_dot(q[:, lo:lo + HEAD_DIM], kv_ref[:, lo:lo + HEAD_DIM])
        e = jnp.exp(s - jnp.max(s, axis=-1, keepdims=True))
        pr = (e / jnp.sum(e, axis=-1, keepdims=True)).astype(BF16)
        heads.append(jnp.dot(pr, kv_ref[:, WIDTH_X + lo:WIDTH_X + lo + HEAD_DIM], preferred_element_type=F32))
    o = jnp.concatenate(heads, axis=1).astype(BF16)
    o_ref[...] = x + jnp.dot(o, wo_ref[...], preferred_element_type=F32)


def _cross(x, g, wq, kv, wo, seq, tm=512):
    T = x.shape[0]
    tm = min(tm, seq)
    per_seq = seq // tm
    return pl.pallas_call(
        _cross_kernel,
        grid=(T // tm,),
        in_specs=[
            pl.BlockSpec((tm, D_MODEL), lambda i: (i, 0)),
            pl.BlockSpec((1, D_MODEL), lambda i: (0, 0)),
            pl.BlockSpec(wq.shape, lambda i: (0, 0)),
            pl.BlockSpec((MEM_LEN, 2 * WIDTH_X), lambda i: (i // per_seq, 0)),
            pl.BlockSpec(wo.shape, lambda i: (0, 0)),
        ],
        out_specs=pl.BlockSpec((tm, D_MODEL), lambda i: (i, 0)),
        out_shape=jax.ShapeDtypeStruct((T, D_MODEL), F32),
        compiler_params=_cparams(("parallel",)),
        name="cross",
    )(x, g, wq, kv, wo)


def _router_kernel(x_ref, g_ref, wr_ref, br_ref, hn_ref, ids_ref, wts_ref, rank_ref, cnt_ref, run_ref):
    i = pl.program_id(0)

    @pl.when(i == 0)
    def _():
        run_ref[...] = jnp.zeros(run_ref.shape, F32)

    hn = _rms(x_ref[...], g_ref[...])
    hn_ref[...] = hn
    logits = jnp.dot(hn, wr_ref[...], preferred_element_type=F32, precision=lax.Precision.HIGHEST) + br_ref[...]
    tm = logits.shape[0]
    lane = lax.broadcasted_iota(jnp.int32, logits.shape, 1)
    lane_f = lane.astype(F32)
    vals = logits
    top_v, top_i, sels = [], [], []
    for _ in range(TOP_K):
        mk = jnp.max(vals, axis=-1, keepdims=True)
        idx = jnp.min(jnp.where(vals == mk, lane_f, float(LANES)), axis=-1, keepdims=True)
        sel = lane_f == idx
        top_v.append(mk)
        top_i.append(idx.astype(jnp.int32))
        sels.append(sel)
        vals = jnp.where(sel, 2.0 * NEG_BIG, vals)
    es = [jnp.exp(v - top_v[0]) for v in top_v]
    denom = es[0] + es[1] + es[2] + es[3]
    onehot = (sels[0] | sels[1] | sels[2] | sels[3]).astype(F32)
    r = lax.broadcasted_iota(jnp.int32, (tm, tm), 0)
    c = lax.broadcasted_iota(jnp.int32, (tm, tm), 1)
    tri = (c < r).astype(BF16)
    prefix = jnp.dot(tri, onehot.astype(BF16), preferred_element_type=F32)
    rank_te = prefix + run_ref[0:1, :]
    ids = jnp.zeros(logits.shape, jnp.int32)
    wts = jnp.zeros(logits.shape, F32)
    rank = jnp.zeros(logits.shape, jnp.int32)
    for k in range(TOP_K):
        rk = jnp.sum(jnp.where(sels[k], rank_te, 0.0), axis=-1, keepdims=True).astype(jnp.int32)
        ids = jnp.where(lane == k, top_i[k], ids)
        wts = jnp.where(lane == k, es[k] / denom, wts)
        rank = jnp.where(lane == k, rk, rank)
    ids_ref[...] = ids
    wts_ref[...] = wts
    rank_ref[...] = rank
    run_ref[...] = run_ref[...] + jnp.sum(onehot, axis=0, keepdims=True)
    cnt_ref[...] = run_ref[...]


def _router(x, g, wr, br, tm=512):
    T = x.shape[0]
    row = pl.BlockSpec((tm, LANES), lambda i: (i, 0))
    return pl.pallas_call(
        _router_kernel,
        grid=(T // tm,),
        in_specs=[pl.BlockSpec((tm, D_MODEL), lambda i: (i, 0)), pl.BlockSpec((1, D_MODEL), lambda i: (0, 0)),
                  pl.BlockSpec((D_MODEL, LANES), lambda i: (0, 0)), pl.BlockSpec((1, LANES), lambda i: (0, 0))],
        out_specs=[pl.BlockSpec((tm, D_MODEL), lambda i: (i, 0)), row, row, row,
                   pl.BlockSpec((8, LANES), lambda i: (0, 0))],
        out_shape=[jax.ShapeDtypeStruct((T, D_MODEL), F32), jax.ShapeDtypeStruct((T, LANES), jnp.int32),
                   jax.ShapeDtypeStruct((T, LANES), F32), jax.ShapeDtypeStruct((T, LANES), jnp.int32),
                   jax.ShapeDtypeStruct((8, LANES), F32)],
        scratch_shapes=[pltpu.VMEM((8, LANES), F32)],
        compiler_params=_cparams(("arbitrary",)),
        name="router",
    )(x, g, wr, br)


DISPATCH_CHUNK = 512


def _dispatch_kernel(gs_ref, ids_ref, rank_ref, hn_ref, xs_in_ref, xs_ref, sem):
    del xs_in_ref
    i = pl.program_id(0)
    base = i * DISPATCH_CHUNK

    def row_copy(src_row, dst_row):
        return pltpu.make_async_copy(hn_ref.at[pl.ds(src_row, 1)], xs_ref.at[pl.ds(dst_row, 1)], sem)

    def issue(c, carry):
        for k in range(TOP_K):
            slot = gs_ref[ids_ref[0, 0, c * TOP_K + k]] + rank_ref[0, 0, c * TOP_K + k]
            row_copy(base + c, slot).start()
        return carry

    def drain(c, carry):
        for k in range(TOP_K):
            row_copy(0, 0).wait()
        return carry

    lax.fori_loop(0, DISPATCH_CHUNK, issue, 0)
    lax.fori_loop(0, DISPATCH_CHUNK, drain, 0)


def _dispatch(hn, ids4, rank4, group_start, n_slots):
    T = hn.shape[0]
    n = T // DISPATCH_CHUNK
    idx_spec = pl.BlockSpec((1, 1, DISPATCH_CHUNK * TOP_K), lambda i, gs: (i, 0, 0), memory_space=pltpu.SMEM)
    xs0 = jnp.zeros((n_slots, D_MODEL), F32)
    return pl.pallas_call(
        _dispatch_kernel,
        grid_spec=pltpu.PrefetchScalarGridSpec(
            num_scalar_prefetch=1,
            grid=(n,),
            in_specs=[idx_spec, idx_spec, pl.BlockSpec(memory_space=pl.ANY), pl.BlockSpec(memory_space=pl.ANY)],
            out_specs=pl.BlockSpec(memory_space=pl.ANY),
            scratch_shapes=[pltpu.SemaphoreType.DMA(())],
        ),
        out_shape=jax.ShapeDtypeStruct((n_slots, D_MODEL), F32),
        input_output_aliases={4: 0},
        compiler_params=_cparams(("arbitrary",)),
        name="dispatch",
    )(group_start, ids4.reshape(n, 1, -1), rank4.reshape(n, 1, -1), hn, xs0)


MOE_TM = 512
MOE_TF = 512


def _experts_kernel(te_ref, tv_ref, xs_ref, w1g_ref, w1l_ref, b1g_ref, b1l_ref, w2_ref, b2_ref, ys_ref, xb_ref, acc_ref):
    i = pl.program_id(0)
    f = pl.program_id(1)
    last = pl.num_programs(1) - 1
    valid = tv_ref[i] > 0

    @pl.when(valid)
    def _():
        @pl.when(f == 0)
        def _():
            xb_ref[...] = xs_ref[...].astype(BF16)
            acc_ref[...] = jnp.zeros(acc_ref.shape, F32)

        xb = xb_ref[...]
        glu = jnp.dot(xb, w1g_ref[0], preferred_element_type=F32) + b1g_ref[0]
        lin = jnp.dot(xb, w1l_ref[0], preferred_element_type=F32) + b1l_ref[0]
        glu = jnp.minimum(glu, SWIGLU_LIMIT)
        lin = jnp.clip(lin, -SWIGLU_LIMIT, SWIGLU_LIMIT)
        act = glu * jax.nn.sigmoid(SWIGLU_ALPHA * glu) * (lin + 1.0)
        acc_ref[...] += jnp.dot(act.astype(BF16), w2_ref[0], preferred_element_type=F32)

        @pl.when(f == last)
        def _():
            ys_ref[...] = acc_ref[...] + b2_ref[0]

    @pl.when(jnp.logical_not(valid) & (f == last))
    def _():
        ys_ref[...] = jnp.zeros(ys_ref.shape, F32)


def _experts(xs, tile_expert, tile_valid, w1g, w1l, b1g, b1l, w2, b2):
    n_slots = xs.shape[0]
    n_tiles = n_slots // MOE_TM
    nf = D_FF // MOE_TF
    return pl.pallas_call(
        _experts_kernel,
        grid_spec=pltpu.PrefetchScalarGridSpec(
            num_scalar_prefetch=2,
            grid=(n_tiles, nf),
            in_specs=[
                pl.BlockSpec((MOE_TM, D_MODEL), lambda i, f, te, tv: (i, 0)),
                pl.BlockSpec((1, D_MODEL, MOE_TF), lambda i, f, te, tv: (te[i], 0, f)),
                pl.BlockSpec((1, D_MODEL, MOE_TF), lambda i, f, te, tv: (te[i], 0, f)),
                pl.BlockSpec((1, 1, MOE_TF), lambda i, f, te, tv: (te[i], 0, f)),
                pl.BlockSpec((1, 1, MOE_TF), lambda i, f, te, tv: (te[i], 0, f)),
                pl.BlockSpec((1, MOE_TF, D_MODEL), lambda i, f, te, tv: (te[i], f, 0)),
                pl.BlockSpec((1, 1, D_MODEL), lambda i, f, te, tv: (te[i], 0, 0)),
            ],
            out_specs=pl.BlockSpec((MOE_TM, D_MODEL), lambda i, f, te, tv: (i, 0)),
            scratch_shapes=[pltpu.VMEM((MOE_TM, D_MODEL), BF16), pltpu.VMEM((MOE_TM, D_MODEL), F32)],
        ),
        out_shape=jax.ShapeDtypeStruct((n_slots, D_MODEL), F32),
        compiler_params=_cparams(("parallel", "arbitrary")),
        name="experts",
    )(tile_expert, tile_valid, xs, w1g, w1l, b1g, b1l, w2, b2)


COMBINE_CHUNK = 128


def _combine_kernel(gs_ref, ids_ref, rank_ref, x_ref, wts_ref, g_ref, ys_ref, o_ref, buf_ref, sem):
    def row_copy(slot, k, c):
        return pltpu.make_async_copy(ys_ref.at[pl.ds(slot, 1)], buf_ref.at[k, pl.ds(c, 1)], sem)

    def issue(c, carry):
        for k in range(TOP_K):
            slot = gs_ref[ids_ref[0, 0, c * TOP_K + k]] + rank_ref[0, 0, c * TOP_K + k]
            row_copy(slot, k, c).start()
        return carry

    def drain(c, carry):
        for k in range(TOP_K):
            row_copy(0, k, c).wait()
        return carry

    lax.fori_loop(0, COMBINE_CHUNK, issue, 0)
    lax.fori_loop(0, COMBINE_CHUNK, drain, 0)
    wts = wts_ref[...]
    y = x_ref[...]
    for k in range(TOP_K):
        y = y + wts[:, k:k + 1] * buf_ref[k]
    o_ref[...] = _rms(y, g_ref[...])


def _combine(x, wts, g, ys, ids4, rank4, group_start):
    T = x.shape[0]
    n = T // COMBINE_CHUNK
    idx_spec = pl.BlockSpec((1, 1, COMBINE_CHUNK * TOP_K), lambda i, gs: (i, 0, 0), memory_space=pltpu.SMEM)
    return pl.pallas_call(
        _combine_kernel,
        grid_spec=pltpu.PrefetchScalarGridSpec(
            num_scalar_prefetch=1,
            grid=(n,),
            in_specs=[idx_spec, idx_spec,
                      pl.BlockSpec((COMBINE_CHUNK, D_MODEL), lambda i, gs: (i, 0)),
                      pl.BlockSpec((COMBINE_CHUNK, LANES), lambda i, gs: (i, 0)),
                      pl.BlockSpec((1, D_MODEL), lambda i, gs: (0, 0)),
                      pl.BlockSpec(memory_space=pl.ANY)],
            out_specs=pl.BlockSpec((COMBINE_CHUNK, D_MODEL), lambda i, gs: (i, 0)),
            scratch_shapes=[pltpu.VMEM((TOP_K, COMBINE_CHUNK, D_MODEL), F32), pltpu.SemaphoreType.DMA(())],
        ),
        out_shape=jax.ShapeDtypeStruct((T, D_MODEL), F32),
        compiler_params=_cparams(("arbitrary",)),
        name="combine",
    )(group_start, ids4.reshape(n, 1, -1), rank4.reshape(n, 1, -1), x, wts, g, ys)


def _encode(x3, mem3, wts):
    batch, seq, _ = x3.shape
    T = batch * seq
    x = x3.reshape(T, D_MODEL)
    mem = mem3.reshape(batch * MEM_LEN, D_MODEL)
    for l in range(DEPTH):
        w = wts[l]
        lam_init = 0.8 - 0.6 * math.exp(-0.3 * l)
        p = _proj(x, w["norm_mix"], w["w_in"], _rope_tables(seq), w["qk_gain"], seq)
        oa = _attn_a(p, batch, seq)
        ob = _attn_b(p, w["lam_vecs"], w["subln_b"], batch, seq, lam_init)
        x = _merge(x, oa, ob, p, w["w_branch_a"], w["w_branch_b"], w["w_out"])
        kv = _norm_matmul(mem, w["norm_mem"], w["w_ckv"])
        x = _cross(x, w["norm_cross"], w["w_cq"], kv, w["w_co"], seq)
        hn, ids, gate_w, rank, counts = _router(x, w["norm_moe"], w["w_router"], w["b_router"])
        cnt = counts[0, :N_EXPERTS].astype(jnp.int32)
        padded = ((cnt + MOE_TM - 1) // MOE_TM) * MOE_TM
        group_end = jnp.cumsum(padded)
        group_start = group_end - padded
        n_tiles = (T * TOP_K) // MOE_TM + N_EXPERTS
        tile_lo = jnp.arange(n_tiles, dtype=jnp.int32) * MOE_TM
        tile_expert = jnp.sum((tile_lo[:, None] >= group_end[None, :]).astype(jnp.int32), axis=1)
        tile_valid = (tile_expert < N_EXPERTS).astype(jnp.int32)
        last_expert = jnp.max(jnp.where(cnt > 0, jnp.arange(N_EXPERTS, dtype=jnp.int32), 0))
        tile_expert = jnp.where(tile_valid > 0, tile_expert, last_expert).astype(jnp.int32)
        ids4 = ids[:, :TOP_K]
        rank4 = rank[:, :TOP_K]
        xs = _dispatch(hn, ids4, rank4, group_start, n_tiles * MOE_TM)
        ys = _experts(xs, tile_expert, tile_valid, w["w1g"], w["w1l"], w["b1g"], w["b1l"], w["w2"], w["b2"])
        x = _combine(x, gate_w, w["norm_final"], ys, ids4, rank4, group_start)
    return x.reshape(batch, seq, D_MODEL)


def kernel(x_prompt, x_sample, mem_prompt, mem_sample, norm_mix, w_in, q_norm_a, k_norm_a, lambda_q1, lambda_k1, lambda_q2, lambda_k2, subln_b, w_branch_a, w_branch_b, w_out, norm_cross, norm_mem, w_cq, w_ckv, w_co, norm_moe, w_router, b_router, w_exp1, b_exp1, w_exp2, b_exp2, norm_final):
    assert DEPTH == 1
    wts = []
    for l in range(DEPTH):
        wr = jnp.zeros((D_MODEL, LANES), F32).at[:, :N_EXPERTS].set(w_router[l])
        br = jnp.full((1, LANES), NEG_BIG, F32).at[0, :N_EXPERTS].set(b_router[l])
        wts.append(dict(
            norm_mix=norm_mix[l][None, :],
            w_in=jnp.concatenate([w_in[l][:, REF_GATE_START:], w_in[l][:, :REF_GATE_START]], axis=1).astype(BF16),
            qk_gain=jnp.stack([q_norm_a[l], k_norm_a[l]]),
            lam_vecs=jnp.stack([lambda_q1[l], lambda_k1[l], lambda_q2[l], lambda_k2[l]]),
            subln_b=subln_b[l][None, :],
            w_branch_a=w_branch_a[l].astype(BF16),
            w_branch_b=w_branch_b[l].astype(BF16),
            w_out=w_out[l].astype(BF16),
            norm_cross=norm_cross[l][None, :],
            norm_mem=norm_mem[l][None, :],
            w_cq=w_cq[l].astype(BF16),
            w_ckv=w_ckv[l].astype(BF16),
            w_co=w_co[l].astype(BF16),
            norm_moe=norm_moe[l][None, :],
            w_router=wr,
            b_router=br,
            w1g=w_exp1[l][:, :, 0::2].astype(BF16),
            w1l=w_exp1[l][:, :, 1::2].astype(BF16),
            b1g=b_exp1[l][:, None, 0::2],
            b1l=b_exp1[l][:, None, 1::2],
            w2=w_exp2[l].astype(BF16),
            b2=b_exp2[l][:, None, :],
            norm_final=norm_final[None, :],
        ))
    y_prompt = _encode(x_prompt, mem_prompt, wts)
    y_sample = _encode(x_sample, mem_sample, wts)
    return (y_prompt, y_sample)
```

```python
import functools
import math

import jax
import jax.numpy as jnp
from jax import lax
from jax.experimental import pallas as pl
from jax.experimental.pallas import tpu as pltpu

F32 = jnp.float32
BF16 = jnp.bfloat16

D_MODEL = 2048
DEPTH = 1
GRID_W = 64
RMS_EPS = 1e-6
N_HEADS_A = 8
N_KV_A = 2
HEAD_DIM = 128
AXIAL_THETA = 10000.0
N_HEADS_B = 4
ROPE_THETA = 500000.0
ROT_DIM_B = HEAD_DIM // 4
MEM_LEN = 256
N_HEADS_X = 4
WIDTH_X = N_HEADS_X * HEAD_DIM
N_EXPERTS = 32
TOP_K = 4
D_FF = D_MODEL
SWIGLU_ALPHA = 1.702
SWIGLU_LIMIT = 7.0
LANES = 128
IN_COLS = 8704
QK_SCALE = HEAD_DIM ** -0.5
QK_SCALE_LOG2E = QK_SCALE * math.log2(math.e)
NEG_BIG = -1e30

VMEM_LIMIT = 56 * 1024 * 1024

COL_GA = 0
COL_GB = 2048
COL_QA = 4096
COL_KA = 5120
COL_VA = 5376
COL_QB = 5632
COL_KB = 6656
COL_VB = 7680
REF_GATE_START = 4608


def _cparams(sem):
    return pltpu.CompilerParams(dimension_semantics=sem, vmem_limit_bytes=VMEM_LIMIT)


def _rms(x, g):
    return x * lax.rsqrt(jnp.mean(x * x, axis=-1, keepdims=True) + RMS_EPS) * g


def _rotate(x, cos, sin_signed, shift):
    n = x.shape[-1]
    lane = lax.broadcasted_iota(jnp.int32, x.shape, 1)
    up = pltpu.roll(x, n - shift, 1)
    dn = pltpu.roll(x, shift, 1)
    partner = jnp.where((lane % (2 * shift)) < shift, up, dn)
    return x * cos + partner * sin_signed


PROJ_TN = 512


def _proj_kernel(x_ref, g_ref, w_ref, ca_ref, sa_ref, cb_ref, sb_ref, qk_ref, o_ref, xn_ref):
    j = pl.program_id(1)

    @pl.when(j == 0)
    def _():
        xn_ref[...] = _rms(x_ref[...], g_ref[...]).astype(BF16)

    acc = jnp.dot(xn_ref[...], w_ref[...], preferred_element_type=F32)
    chunks = [acc[:, c * LANES:(c + 1) * LANES] for c in range(PROJ_TN // LANES)]

    def axial(a, gain, scale):
        r = _rotate(_rms(a, gain), ca_ref[...], sa_ref[...], HEAD_DIM // 4)
        return r * scale if scale != 1.0 else r

    def partial_rot(a, scale):
        r = _rotate(a, cb_ref[...], sb_ref[...], ROT_DIM_B // 2)
        return r * scale if scale != 1.0 else r

    def store(vals):
        for c, v in enumerate(vals):
            o_ref[:, c * LANES:(c + 1) * LANES] = v.astype(o_ref.dtype)

    @pl.when(j < 8)
    def _():
        store([jax.nn.sigmoid(a) for a in chunks])

    @pl.when((j >= 8) & (j < 10))
    def _():
        store([axial(a, qk_ref[0:1, :], QK_SCALE_LOG2E) for a in chunks])

    @pl.when(j == 10)
    def _():
        store([axial(chunks[0], qk_ref[1:2, :], 1.0), axial(chunks[1], qk_ref[1:2, :], 1.0), chunks[2], chunks[3]])

    @pl.when((j >= 11) & (j < 13))
    def _():
        store([partial_rot(a, QK_SCALE_LOG2E) for a in chunks])

    @pl.when((j >= 13) & (j < 15))
    def _():
        store([partial_rot(a, 1.0) for a in chunks])

    @pl.when(j >= 15)
    def _():
        store(chunks)


def _rope_tables(seq):
    t = jnp.arange(seq, dtype=jnp.int32)
    row = (t // GRID_W).astype(F32)
    col = (t % GRID_W).astype(F32)
    lane = jnp.arange(LANES, dtype=jnp.int32)
    half = HEAD_DIM // 4
    inv_a = jnp.power(AXIAL_THETA, -2.0 * (lane % half).astype(F32) / (HEAD_DIM // 2))
    pos_a = jnp.where(lane[None, :] < HEAD_DIM // 2, row[:, None], col[:, None])
    ang_a = pos_a * inv_a[None, :]
    sign_a = jnp.where((lane % (2 * half)) < half, -1.0, 1.0).astype(F32)
    ca, sa = jnp.cos(ang_a), jnp.sin(ang_a) * sign_a[None, :]
    hb = ROT_DIM_B // 2
    inv_b = jnp.power(ROPE_THETA, -2.0 * (lane % hb).astype(F32) / ROT_DIM_B)
    ang_b = t.astype(F32)[:, None] * inv_b[None, :]
    rot = (lane < ROT_DIM_B)[None, :]
    sign_b = jnp.where((lane % (2 * hb)) < hb, -1.0, 1.0).astype(F32)
    cb = jnp.where(rot, jnp.cos(ang_b), 1.0)
    sb = jnp.where(rot, jnp.sin(ang_b) * sign_b[None, :], 0.0)
    return ca, sa, cb, sb


def _proj(x, g, w, tables, qk_gain, seq, tm=1024):
    T = x.shape[0]
    tm = min(tm, seq)
    nseq = seq // tm
    tab_spec = pl.BlockSpec((tm, LANES), lambda i, j: (i % nseq, 0))
    return pl.pallas_call(
        _proj_kernel,
        grid=(T // tm, IN_COLS // PROJ_TN),
        in_specs=[
            pl.BlockSpec((tm, D_MODEL), lambda i, j: (i, 0)),
            pl.BlockSpec((1, D_MODEL), lambda i, j: (0, 0)),
            pl.BlockSpec((D_MODEL, PROJ_TN), lambda i, j: (0, j)),
            tab_spec, tab_spec, tab_spec, tab_spec,
            pl.BlockSpec((2, LANES), lambda i, j: (0, 0)),
        ],
        out_specs=pl.BlockSpec((tm, PROJ_TN), lambda i, j: (i, j)),
        out_shape=jax.ShapeDtypeStruct((T, IN_COLS), BF16),
        scratch_shapes=[pltpu.VMEM((tm, D_MODEL), BF16)],
        compiler_params=_cparams(("parallel", "arbitrary")),
        name="proj",
    )(x, g, w, *tables, qk_gain)


def _nt_dot(a, b):
    return lax.dot_general(a, b, (((1,), (1,)), ((), ())), preferred_element_type=F32)


def _lane_tile(x, n):
    return jnp.concatenate([x] * n, axis=1)


def _softmax_block(s, m_ref):
    m_prev = m_ref[...]
    m_new = jnp.maximum(m_prev, jnp.max(s, axis=-1, keepdims=True))
    alpha = jnp.exp2(m_prev - m_new)
    p = jnp.exp2(s - _lane_tile(m_new, s.shape[1] // LANES))
    m_ref[...] = m_new
    return alpha, p


def _attn_a_kernel(q_ref, k_ref, v_ref, o_ref, m_ref, acc_ref, vaug_ref, *, tk):
    tq = q_ref.shape[0]
    group = N_HEADS_A // N_KV_A

    @pl.when(pl.program_id(2) == 0)
    def _():
        vaug_ref[:, :HEAD_DIM] = v_ref[...]
        vaug_ref[:, HEAD_DIM:] = jnp.ones(v_ref.shape, BF16)

    m_ref[...] = jnp.full(m_ref.shape, NEG_BIG, F32)
    acc_ref[...] = jnp.zeros(acc_ref.shape, F32)

    def body(i, carry):
        off = pl.multiple_of(i * tk, tk)
        k = k_ref[pl.ds(off, tk), :]
        v = vaug_ref[pl.ds(off, tk), :]
        for g in range(group):
            rows = pl.ds(g * tq, tq)
            s = _nt_dot(q_ref[:, g * HEAD_DIM:(g + 1) * HEAD_DIM], k)
            alpha, p = _softmax_block(s, m_ref.at[rows])
            pv = jnp.dot(p.astype(BF16), v, preferred_element_type=F32)
            acc_ref[rows, :] = _lane_tile(alpha, 2) * acc_ref[rows, :] + pv
        return carry

    lax.fori_loop(0, k_ref.shape[0] // tk, body, 0)
    o = acc_ref[:, :HEAD_DIM] / acc_ref[:, HEAD_DIM:]
    for g in range(group):
        o_ref[:, g * HEAD_DIM:(g + 1) * HEAD_DIM] = o[g * tq:(g + 1) * tq].astype(o_ref.dtype)


def _single_buffered(shape, index_map):
    return pl.BlockSpec(shape, index_map, pipeline_mode=pl.Buffered(1))


def _attn_a(p, batch, seq, tq=512, tk=2048):
    T = p.shape[0]
    group = N_HEADS_A // N_KV_A
    qw = group * HEAD_DIM
    tq = min(tq, seq)
    tk = min(tk, seq)
    nq = seq // tq
    return pl.pallas_call(
        functools.partial(_attn_a_kernel, tk=tk),
        grid=(batch, N_KV_A, nq),
        in_specs=[
            pl.BlockSpec((tq, qw), lambda b, h, i: (b * nq + i, COL_QA // qw + h)),
            _single_buffered((seq, HEAD_DIM), lambda b, h, i: (b, COL_KA // HEAD_DIM + h)),
            _single_buffered((seq, HEAD_DIM), lambda b, h, i: (b, COL_VA // HEAD_DIM + h)),
        ],
        out_specs=pl.BlockSpec((tq, qw), lambda b, h, i: (b * nq + i, h)),
        out_shape=jax.ShapeDtypeStruct((T, N_HEADS_A * HEAD_DIM), BF16),
        scratch_shapes=[pltpu.VMEM((group * tq, LANES), F32),
                        pltpu.VMEM((group * tq, 2 * HEAD_DIM), F32),
                        pltpu.VMEM((seq, 2 * HEAD_DIM), BF16)],
        compiler_params=_cparams(("parallel", "parallel", "arbitrary")),
        name="attn_a",
    )(p, p, p)


def _attn_b_kernel(lam_ref, q_ref, k_ref, v_ref, sub_ref, o_ref, m_ref, l_ref, acc_ref, *, tk, sub, lam_init):
    tq = q_ref.shape[0]
    m_ref[...] = jnp.full(m_ref.shape, NEG_BIG, F32)
    l_ref[...] = jnp.zeros(l_ref.shape, F32)
    acc_ref[...] = jnp.zeros(acc_ref.shape, F32)

    def body(i, carry):
        off = pl.multiple_of(i * tk, tk)
        v = v_ref[pl.ds(off, tk), :]
        for c in range(2):
            cols = slice(c * HEAD_DIM, (c + 1) * HEAD_DIM)
            k = k_ref[pl.ds(off, tk), cols]
            for r in range(tq // sub):
                rows = pl.ds(c * tq + r * sub, sub)
                s = _nt_dot(q_ref[r * sub:(r + 1) * sub, cols], k)
                alpha, p = _softmax_block(s, m_ref.at[rows])
                l_ref[rows, :] = alpha * l_ref[rows, :] + jnp.sum(p, axis=-1, keepdims=True)
                pv = jnp.dot(p.astype(BF16), v, preferred_element_type=F32)
                acc_ref[rows, :] = _lane_tile(alpha, 2) * acc_ref[rows, :] + pv
        return carry

    lax.fori_loop(0, k_ref.shape[0] // tk, body, 0)
    lv = lam_ref[...]
    lam = (jnp.exp(jnp.sum(lv[0:1] * lv[1:2], axis=-1, keepdims=True))
           - jnp.exp(jnp.sum(lv[2:3] * lv[3:4], axis=-1, keepdims=True)) + lam_init)
    o = acc_ref[...] / _lane_tile(l_ref[...], 2)
    od = o[:tq] - lam * o[tq:]
    o_ref[...] = (_rms(od, sub_ref[...]) * (1.0 - lam_init)).astype(o_ref.dtype)


def _attn_b(p, lam_vecs, subln, batch, seq, lam_init, tq=1024, sub=512, tk=2048):
    T = p.shape[0]
    vw = 2 * HEAD_DIM
    tq = min(tq, seq)
    sub = min(sub, tq)
    tk = min(tk, seq)
    nq = seq // tq
    return pl.pallas_call(
        functools.partial(_attn_b_kernel, tk=tk, sub=sub, lam_init=lam_init),
        grid=(batch, N_HEADS_B, nq),
        in_specs=[
            pl.BlockSpec((4, HEAD_DIM), lambda b, h, i: (0, 0)),
            pl.BlockSpec((tq, vw), lambda b, h, i: (b * nq + i, COL_QB // vw + h)),
            _single_buffered((seq, vw), lambda b, h, i: (b, COL_KB // vw + h)),
            _single_buffered((seq, vw), lambda b, h, i: (b, COL_VB // vw + h)),
            pl.BlockSpec((1, vw), lambda b, h, i: (0, 0)),
        ],
        out_specs=pl.BlockSpec((tq, vw), lambda b, h, i: (b * nq + i, h)),
        out_shape=jax.ShapeDtypeStruct((T, N_HEADS_B * vw), BF16),
        scratch_shapes=[pltpu.VMEM((2 * tq, LANES), F32), pltpu.VMEM((2 * tq, LANES), F32),
                        pltpu.VMEM((2 * tq, vw), F32)],
        compiler_params=_cparams(("parallel", "parallel", "parallel")),
        name="attn_b",
    )(lam_vecs, p, p, p, subln)


def _merge_kernel(x_ref, oa_ref, ob_ref, ga_ref, gb_ref, wa_ref, wb_ref, wo_ref, o_ref):
    a = jnp.dot(oa_ref[...], wa_ref[...], preferred_element_type=F32)
    b = jnp.dot(ob_ref[...], wb_ref[...], preferred_element_type=F32)
    merged = ga_ref[...].astype(F32) * a + gb_ref[...].astype(F32) * b
    o_ref[...] = x_ref[...] + jnp.dot(merged.astype(BF16), wo_ref[...], preferred_element_type=F32)


def _resident(shape):
    return pl.BlockSpec(shape, lambda *_: (0,) * len(shape), pipeline_mode=pl.Buffered(1))


def _merge(x, oa, ob, p, wa, wb, wo, tm=256):
    T = x.shape[0]
    return pl.pallas_call(
        _merge_kernel,
        grid=(T // tm,),
        in_specs=[
            pl.BlockSpec((tm, D_MODEL), lambda i: (i, 0)),
            pl.BlockSpec((tm, oa.shape[1]), lambda i: (i, 0)),
            pl.BlockSpec((tm, ob.shape[1]), lambda i: (i, 0)),
            pl.BlockSpec((tm, D_MODEL), lambda i: (i, COL_GA // D_MODEL)),
            pl.BlockSpec((tm, D_MODEL), lambda i: (i, COL_GB // D_MODEL)),
            _resident(wa.shape), _resident(wb.shape), _resident(wo.shape),
        ],
        out_specs=pl.BlockSpec((tm, D_MODEL), lambda i: (i, 0)),
        out_shape=jax.ShapeDtypeStruct((T, D_MODEL), F32),
        compiler_params=_cparams(("parallel",)),
        name="merge",
    )(x, oa, ob, p, p, wa, wb, wo)


def _norm_matmul_kernel(x_ref, g_ref, w_ref, o_ref):
    xn = _rms(x_ref[...], g_ref[...]).astype(BF16)
    o_ref[...] = jnp.dot(xn, w_ref[...], preferred_element_type=F32).astype(o_ref.dtype)


def _norm_matmul(x, g, w, tm=256):
    T, K = x.shape
    N = w.shape[1]
    return pl.pallas_call(
        _norm_matmul_kernel,
        grid=(T // tm,),
        in_specs=[pl.BlockSpec((tm, K), lambda i: (i, 0)), pl.BlockSpec((1, K), lambda i: (0, 0)),
                  pl.BlockSpec((K, N), lambda i: (0, 0))],
        out_specs=pl.BlockSpec((tm, N), lambda i: (i, 0)),
        out_shape=jax.ShapeDtypeStruct((T, N), BF16),
        compiler_params=_cparams(("parallel",)),
        name="mem_kv",
    )(x, g, w)


def _cross_kernel(x_ref, g_ref, wq_ref, kv_ref, wo_ref, o_ref):
    x = x_ref[...]
    hn = _rms(x, g_ref[...]).astype(BF16)
    q = (jnp.dot(hn, wq_ref[...], preferred_element_type=F32) * QK_SCALE).astype(BF16)
    heads = []
    for h in range(N_HEADS_X):
        lo = h * HEAD_DIM
        s = _nt_dot(q[:, lo:lo + HEAD_DIM], kv_ref[:, lo:lo + HEAD_DIM])
        e = jnp.exp(s - jnp.max(s, axis=-1, keepdims=True))
        pr = (e / jnp.sum(e, axis=-1, keepdims=True)).astype(BF16)
        heads.append(jnp.dot(pr, kv_ref[:, WIDTH_X + lo:WIDTH_X + lo + HEAD_DIM], preferred_element_type=F32))
    o = jnp.concatenate(heads, axis=1).astype(BF16)
    o_ref[...] = x + jnp.dot(o, wo_ref[...], preferred_element_type=F32)


def _cross(x, g, wq, kv, wo, seq, tm=512):
    T = x.shape[0]
    tm = min(tm, seq)
    per_seq = seq // tm
    return pl.pallas_call(
        _cross_kernel,
        grid=(T // tm,),
        in_specs=[
            pl.BlockSpec((tm, D_MODEL), lambda i: (i, 0)),
            pl.BlockSpec((1, D_MODEL), lambda i: (0, 0)),
            pl.BlockSpec(wq.shape, lambda i: (0, 0)),
            pl.BlockSpec((MEM_LEN, 2 * WIDTH_X), lambda i: (i // per_seq, 0)),
            pl.BlockSpec(wo.shape, lambda i: (0, 0)),
        ],
        out_specs=pl.BlockSpec((tm, D_MODEL), lambda i: (i, 0)),
        out_shape=jax.ShapeDtypeStruct((T, D_MODEL), F32),
        compiler_params=_cparams(("parallel",)),
        name="cross",
    )(x, g, wq, kv, wo)


def _router_kernel(x_ref, g_ref, wr_ref, br_ref, hn_ref, ids_ref, wts_ref, rank_ref, cnt_ref, run_ref):
    i = pl.program_id(0)

    @pl.when(i == 0)
    def _():
        run_ref[...] = jnp.zeros(run_ref.shape, F32)

    hn = _rms(x_ref[...], g_ref[...])
    hn_ref[...] = hn
    logits = jnp.dot(hn, wr_ref[...], preferred_element_type=F32, precision=lax.Precision.HIGHEST) + br_ref[...]
    tm = logits.shape[0]
    lane = lax.broadcasted_iota(jnp.int32, logits.shape, 1)
    lane_f = lane.astype(F32)
    vals = logits
    top_v, top_i, sels = [], [], []
    for _ in range(TOP_K):
        mk = jnp.max(vals, axis=-1, keepdims=True)
        idx = jnp.min(jnp.where(vals == mk, lane_f, float(LANES)), axis=-1, keepdims=True)
        sel = lane_f == idx
        top_v.append(mk)
        top_i.append(idx.astype(jnp.int32))
        sels.append(sel)
        vals = jnp.where(sel, 2.0 * NEG_BIG, vals)
    es = [jnp.exp(v - top_v[0]) for v in top_v]
    denom = es[0] + es[1] + es[2] + es[3]
    onehot = (sels[0] | sels[1] | sels[2] | sels[3]).astype(F32)
    r = lax.broadcasted_iota(jnp.int32, (tm, tm), 0)
    c = lax.broadcasted_iota(jnp.int32, (tm, tm), 1)
    tri = (c < r).astype(BF16)
    prefix = jnp.dot(tri, onehot.astype(BF16), preferred_element_type=F32)
    rank_te = prefix + run_ref[0:1, :]
    ids = jnp.zeros(logits.shape, jnp.int32)
    wts = jnp.zeros(logits.shape, F32)
    rank = jnp.zeros(logits.shape, jnp.int32)
    for k in range(TOP_K):
        rk = jnp.sum(jnp.where(sels[k], rank_te, 0.0), axis=-1, keepdims=True).astype(jnp.int32)
        ids = jnp.where(lane == k, top_i[k], ids)
        wts = jnp.where(lane == k, es[k] / denom, wts)
        rank = jnp.where(lane == k, rk, rank)
    ids_ref[...] = ids
    wts_ref[...] = wts
    rank_ref[...] = rank
    run_ref[...] = run_ref[...] + jnp.sum(onehot, axis=0, keepdims=True)
    cnt_ref[...] = run_ref[...]


def _router(x, g, wr, br, tm=512):
    T = x.shape[0]
    row = pl.BlockSpec((tm, LANES), lambda i: (i, 0))
    return pl.pallas_call(
        _router_kernel,
        grid=(T // tm,),
        in_specs=[pl.BlockSpec((tm, D_MODEL), lambda i: (i, 0)), pl.BlockSpec((1, D_MODEL), lambda i: (0, 0)),
                  pl.BlockSpec((D_MODEL, LANES), lambda i: (0, 0)), pl.BlockSpec((1, LANES), lambda i: (0, 0))],
        out_specs=[pl.BlockSpec((tm, D_MODEL), lambda i: (i, 0)), row, row, row,
                   pl.BlockSpec((8, LANES), lambda i: (0, 0))],
        out_shape=[jax.ShapeDtypeStruct((T, D_MODEL), F32), jax.ShapeDtypeStruct((T, LANES), jnp.int32),
                   jax.ShapeDtypeStruct((T, LANES), F32), jax.ShapeDtypeStruct((T, LANES), jnp.int32),
                   jax.ShapeDtypeStruct((8, LANES), F32)],
        scratch_shapes=[pltpu.VMEM((8, LANES), F32)],
        compiler_params=_cparams(("arbitrary",)),
        name="router",
    )(x, g, wr, br)


DISPATCH_CHUNK = 512


def _dispatch_kernel(gs_ref, ids_ref, rank_ref, hn_ref, xs_in_ref, xs_ref, sem):
    del xs_in_ref

    def row_copy(src_row, dst_row):
        return pltpu.make_async_copy(hn_ref.at[pl.ds(src_row, 1)], xs_ref.at[pl.ds(dst_row, 1)], sem)

    def issue(c, carry):
        for k in range(TOP_K):
            slot = gs_ref[ids_ref[0, 0, c * TOP_K + k]] + rank_ref[0, 0, c * TOP_K + k]
            row_copy(c, slot).start()
        return carry

    def drain(c, carry):
        for k in range(TOP_K):
            row_copy(0, 0).wait()
        return carry

    lax.fori_loop(0, DISPATCH_CHUNK, issue, 0)
    lax.fori_loop(0, DISPATCH_CHUNK, drain, 0)


def _dispatch(hn, ids4, rank4, group_start, n_slots):
    T = hn.shape[0]
    n = T // DISPATCH_CHUNK
    idx_spec = pl.BlockSpec((1, 1, DISPATCH_CHUNK * TOP_K), lambda i, gs: (i, 0, 0), memory_space=pltpu.SMEM)
    xs0 = jnp.zeros((n_slots, D_MODEL), F32)
    return pl.pallas_call(
        _dispatch_kernel,
        grid_spec=pltpu.PrefetchScalarGridSpec(
            num_scalar_prefetch=1,
            grid=(n,),
            in_specs=[idx_spec, idx_spec, pl.BlockSpec((DISPATCH_CHUNK, D_MODEL), lambda i, gs: (i, 0)),
                      pl.BlockSpec(memory_space=pl.ANY)],
            out_specs=pl.BlockSpec(memory_space=pl.ANY),
            scratch_shapes=[pltpu.SemaphoreType.DMA(())],
        ),
        out_shape=jax.ShapeDtypeStruct((n_slots, D_MODEL), F32),
        input_output_aliases={4: 0},
        compiler_params=_cparams(("arbitrary",)),
        name="dispatch",
    )(group_start, ids4.reshape(n, 1, -1), rank4.reshape(n, 1, -1), hn, xs0)


SPLIT_TN = 512


def _split_glu_kernel(w_ref, perm_ref, g_ref, l_ref):
    r = jnp.dot(w_ref[0].astype(BF16), perm_ref[...], preferred_element_type=F32)
    half = SPLIT_TN // 2
    g_ref[0] = r[:, :half].astype(BF16)
    l_ref[0] = r[:, half:].astype(BF16)


def _split_glu(w1):
    E, K, N2 = w1.shape
    half = SPLIT_TN // 2
    src = jnp.arange(SPLIT_TN, dtype=jnp.int32)
    dst = jnp.where(src % 2 == 0, src // 2, half + src // 2)
    perm = (dst[:, None] == jnp.arange(SPLIT_TN, dtype=jnp.int32)[None, :]).astype(BF16)
    out = jax.ShapeDtypeStruct((E, K, N2 // 2), BF16)
    return pl.pallas_call(
        _split_glu_kernel,
        grid=(E, N2 // SPLIT_TN),
        in_specs=[pl.BlockSpec((1, K, SPLIT_TN), lambda e, j: (e, 0, j)),
                  pl.BlockSpec((SPLIT_TN, SPLIT_TN), lambda e, j: (0, 0))],
        out_specs=[pl.BlockSpec((1, K, half), lambda e, j: (e, 0, j)),
                   pl.BlockSpec((1, K, half), lambda e, j: (e, 0, j))],
        out_shape=[out, out],
        compiler_params=_cparams(("parallel", "parallel")),
        name="split_glu",
    )(w1, perm)


MOE_TM = 512
MOE_TF = 512


def _experts_kernel(te_ref, tv_ref, xs_ref, w1g_ref, w1l_ref, b1g_ref, b1l_ref, w2_ref, b2_ref, ys_ref, xb_ref, acc_ref):
    i = pl.program_id(0)
    f = pl.program_id(1)
    last = pl.num_programs(1) - 1
    valid = tv_ref[i] > 0

    @pl.when(valid)
    def _():
        @pl.when(f == 0)
        def _():
            xb_ref[...] = xs_ref[...].astype(BF16)
            acc_ref[...] = jnp.zeros(acc_ref.shape, F32)

        xb = xb_ref[...]
        glu = jnp.dot(xb, w1g_ref[0], preferred_element_type=F32) + b1g_ref[0]
        lin = jnp.dot(xb, w1l_ref[0], preferred_element_type=F32) + b1l_ref[0]
        glu = jnp.minimum(glu, SWIGLU_LIMIT)
        lin = jnp.clip(lin, -SWIGLU_LIMIT, SWIGLU_LIMIT)
        act = glu * jax.nn.sigmoid(SWIGLU_ALPHA * glu) * (lin + 1.0)
        acc_ref[...] += jnp.dot(act.astype(BF16), w2_ref[0], preferred_element_type=F32)

        @pl.when(f == last)
        def _():
            ys_ref[...] = acc_ref[...] + b2_ref[0]

    @pl.when(jnp.logical_not(valid) & (f == last))
    def _():
        ys_ref[...] = jnp.zeros(ys_ref.shape, F32)


def _experts(xs, tile_expert, tile_valid, w1g, w1l, b1g, b1l, w2, b2):
    n_slots = xs.shape[0]
    n_tiles = n_slots // MOE_TM
    nf = D_FF // MOE_TF
    return pl.pallas_call(
        _experts_kernel,
        grid_spec=pltpu.PrefetchScalarGridSpec(
            num_scalar_prefetch=2,
            grid=(n_tiles, nf),
            in_specs=[
                pl.BlockSpec((MOE_TM, D_MODEL), lambda i, f, te, tv: (i, 0)),
                pl.BlockSpec((1, D_MODEL, MOE_TF), lambda i, f, te, tv: (te[i], 0, f)),
                pl.BlockSpec((1, D_MODEL, MOE_TF), lambda i, f, te, tv: (te[i], 0, f)),
                pl.BlockSpec((1, 1, MOE_TF), lambda i, f, te, tv: (te[i], 0, f)),
                pl.BlockSpec((1, 1, MOE_TF), lambda i, f, te, tv: (te[i], 0, f)),
                pl.BlockSpec((1, MOE_TF, D_MODEL), lambda i, f, te, tv: (te[i], f, 0)),
                pl.BlockSpec((1, 1, D_MODEL), lambda i, f, te, tv: (te[i], 0, 0)),
            ],
            out_specs=pl.BlockSpec((MOE_TM, D_MODEL), lambda i, f, te, tv: (i, 0)),
            scratch_shapes=[pltpu.VMEM((MOE_TM, D_MODEL), BF16), pltpu.VMEM((MOE_TM, D_MODEL), F32)],
        ),
        out_shape=jax.ShapeDtypeStruct((n_slots, D_MODEL), F32),
        compiler_params=_cparams(("parallel", "arbitrary")),
        name="experts",
    )(tile_expert, tile_valid, xs, w1g, w1l, b1g, b1l, w2, b2)


COMBINE_CHUNK = 128


def _combine_kernel(gs_ref, ids_ref, rank_ref, x_ref, wts_ref, g_ref, ys_ref, o_ref, buf_ref, sem):
    def row_copy(slot, k, c):
        return pltpu.make_async_copy(ys_ref.at[pl.ds(slot, 1)], buf_ref.at[k, pl.ds(c, 1)], sem)

    def issue(c, carry):
        for k in range(TOP_K):
            slot = gs_ref[ids_ref[0, 0, c * TOP_K + k]] + rank_ref[0, 0, c * TOP_K + k]
            row_copy(slot, k, c).start()
        return carry

    def drain(c, carry):
        for k in range(TOP_K):
            row_copy(0, k, c).wait()
        return carry

    lax.fori_loop(0, COMBINE_CHUNK, issue, 0)
    lax.fori_loop(0, COMBINE_CHUNK, drain, 0)
    wts = wts_ref[...]
    y = x_ref[...]
    for k in range(TOP_K):
        y = y + wts[:, k:k + 1] * buf_ref[k]
    o_ref[...] = _rms(y, g_ref[...])


def _combine(x, wts, g, ys, ids4, rank4, group_start):
    T = x.shape[0]
    n = T // COMBINE_CHUNK
    idx_spec = pl.BlockSpec((1, 1, COMBINE_CHUNK * TOP_K), lambda i, gs: (i, 0, 0), memory_space=pltpu.SMEM)
    return pl.pallas_call(
        _combine_kernel,
        grid_spec=pltpu.PrefetchScalarGridSpec(
            num_scalar_prefetch=1,
            grid=(n,),
            in_specs=[idx_spec, idx_spec,
                      pl.BlockSpec((COMBINE_CHUNK, D_MODEL), lambda i, gs: (i, 0)),
                      pl.BlockSpec((COMBINE_CHUNK, LANES), lambda i, gs: (i, 0)),
                      pl.BlockSpec((1, D_MODEL), lambda i, gs: (0, 0)),
                      pl.BlockSpec(memory_space=pl.ANY)],
            out_specs=pl.BlockSpec((COMBINE_CHUNK, D_MODEL), lambda i, gs: (i, 0)),
            scratch_shapes=[pltpu.VMEM((TOP_K, COMBINE_CHUNK, D_MODEL), F32), pltpu.SemaphoreType.DMA(())],
        ),
        out_shape=jax.ShapeDtypeStruct((T, D_MODEL), F32),
        compiler_params=_cparams(("arbitrary",)),
        name="combine",
    )(group_start, ids4.reshape(n, 1, -1), rank4.reshape(n, 1, -1), x, wts, g, ys)


def _encode(x3, mem3, wts):
    batch, seq, _ = x3.shape
    T = batch * seq
    x = x3.reshape(T, D_MODEL)
    mem = mem3.reshape(batch * MEM_LEN, D_MODEL)
    for l in range(DEPTH):
        w = wts[l]
        lam_init = 0.8 - 0.6 * math.exp(-0.3 * l)
        p = _proj(x, w["norm_mix"], w["w_in"], _rope_tables(seq), w["qk_gain"], seq)
        oa = _attn_a(p, batch, seq)
        ob = _attn_b(p, w["lam_vecs"], w["subln_b"], batch, seq, lam_init)
        x = _merge(x, oa, ob, p, w["w_branch_a"], w["w_branch_b"], w["w_out"])
        kv = _norm_matmul(mem, w["norm_mem"], w["w_ckv"])
        x = _cross(x, w["norm_cross"], w["w_cq"], kv, w["w_co"], seq)
        hn, ids, gate_w, rank, counts = _router(x, w["norm_moe"], w["w_router"], w["b_router"])
        cnt = counts[0, :N_EXPERTS].astype(jnp.int32)
        padded = ((cnt + MOE_TM - 1) // MOE_TM) * MOE_TM
        group_end = jnp.cumsum(padded)
        group_start = group_end - padded
        n_tiles = (T * TOP_K) // MOE_TM + N_EXPERTS
        tile_lo = jnp.arange(n_tiles, dtype=jnp.int32) * MOE_TM
        tile_expert = jnp.sum((tile_lo[:, None] >= group_end[None, :]).astype(jnp.int32), axis=1)
        tile_valid = (tile_expert < N_EXPERTS).astype(jnp.int32)
        last_expert = jnp.max(jnp.where(cnt > 0, jnp.arange(N_EXPERTS, dtype=jnp.int32), 0))
        tile_expert = jnp.where(tile_valid > 0, tile_expert, last_expert).astype(jnp.int32)
        ids4 = ids[:, :TOP_K]
        rank4 = rank[:, :TOP_K]
        xs = _dispatch(hn, ids4, rank4, group_start, n_tiles * MOE_TM)
        ys = _experts(xs, tile_expert, tile_valid, w["w1g"], w["w1l"], w["b1g"], w["b1l"], w["w2"], w["b2"])
        x = _combine(x, gate_w, w["norm_final"], ys, ids4, rank4, group_start)
    return x.reshape(batch, seq, D_MODEL)


def kernel(x_prompt, x_sample, mem_prompt, mem_sample, norm_mix, w_in, q_norm_a, k_norm_a, lambda_q1, lambda_k1, lambda_q2, lambda_k2, subln_b, w_branch_a, w_branch_b, w_out, norm_cross, norm_mem, w_cq, w_ckv, w_co, norm_moe, w_router, b_router, w_exp1, b_exp1, w_exp2, b_exp2, norm_final):
    assert DEPTH == 1
    wts = []
    for l in range(DEPTH):
        wr = jnp.zeros((D_MODEL, LANES), F32).at[:, :N_EXPERTS].set(w_router[l])
        br = jnp.full((1, LANES), NEG_BIG, F32).at[0, :N_EXPERTS].set(b_router[l])
        w1g, w1l = _split_glu(w_exp1[l])
        wts.append(dict(
            norm_mix=norm_mix[l][None, :],
            w_in=jnp.concatenate([w_in[l][:, REF_GATE_START:], w_in[l][:, :REF_GATE_START]], axis=1).astype(BF16),
            qk_gain=jnp.stack([q_norm_a[l], k_norm_a[l]]),
            lam_vecs=jnp.stack([lambda_q1[l], lambda_k1[l], lambda_q2[l], lambda_k2[l]]),
            subln_b=subln_b[l][None, :],
            w_branch_a=w_branch_a[l].astype(BF16),
            w_branch_b=w_branch_b[l].astype(BF16),
            w_out=w_out[l].astype(BF16),
            norm_cross=norm_cross[l][None, :],
            norm_mem=norm_mem[l][None, :],
            w_cq=w_cq[l].astype(BF16),
            w_ckv=w_ckv[l].astype(BF16),
            w_co=w_co[l].astype(BF16),
            norm_moe=norm_moe[l][None, :],
            w_router=wr,
            b_router=br,
            w1g=w1g,
            w1l=w1l,
            b1g=b_exp1[l][:, None, 0::2],
            b1l=b_exp1[l][:, None, 1::2],
            w2=w_exp2[l].astype(BF16),
            b2=b_exp2[l][:, None, :],
            norm_final=norm_final[None, :],
        ))
    y_prompt = _encode(x_prompt, mem_prompt, wts)
    y_sample = _encode(x_sample, mem_sample, wts)
    return (y_prompt, y_sample)
```

```python
import functools
import math

import jax
import jax.numpy as jnp
from jax import lax
from jax.experimental import pallas as pl
from jax.experimental.pallas import tpu as pltpu

F32 = jnp.float32
BF16 = jnp.bfloat16

D_MODEL = 2048
DEPTH = 1
GRID_W = 64
RMS_EPS = 1e-6
N_HEADS_A = 8
N_KV_A = 2
HEAD_DIM = 128
AXIAL_THETA = 10000.0
N_HEADS_B = 4
ROPE_THETA = 500000.0
ROT_DIM_B = HEAD_DIM // 4
MEM_LEN = 256
N_HEADS_X = 4
WIDTH_X = N_HEADS_X * HEAD_DIM
N_EXPERTS = 32
TOP_K = 4
D_FF = D_MODEL
SWIGLU_ALPHA = 1.702
SWIGLU_LIMIT = 7.0
LANES = 128
IN_COLS = 8704
QK_SCALE = HEAD_DIM ** -0.5
QK_SCALE_LOG2E = QK_SCALE * math.log2(math.e)
NEG_BIG = -1e30

VMEM_LIMIT = 56 * 1024 * 1024

COL_GA = 0
COL_GB = 2048
COL_QA = 4096
COL_KA = 5120
COL_VA = 5376
COL_QB = 5632
COL_KB = 6656
COL_VB = 7680
REF_GATE_START = 4608


def _cparams(sem):
    return pltpu.CompilerParams(dimension_semantics=sem, vmem_limit_bytes=VMEM_LIMIT)


def _rms(x, g):
    return x * lax.rsqrt(jnp.mean(x * x, axis=-1, keepdims=True) + RMS_EPS) * g


def _rotate(x, cos, sin_signed, shift):
    n = x.shape[-1]
    lane = lax.broadcasted_iota(jnp.int32, x.shape, 1)
    up = pltpu.roll(x, n - shift, 1)
    dn = pltpu.roll(x, shift, 1)
    partner = jnp.where((lane % (2 * shift)) < shift, up, dn)
    return x * cos + partner * sin_signed


PROJ_TN = 512


def _proj_kernel(x_ref, g_ref, w_ref, ca_ref, sa_ref, cb_ref, sb_ref, qk_ref, o_ref, xn_ref):
    j = pl.program_id(1)

    @pl.when(j == 0)
    def _():
        xn_ref[...] = _rms(x_ref[...], g_ref[...]).astype(BF16)

    tm = xn_ref.shape[0]
    n_sub = 2 if tm % 16 == 0 else 1
    sub = tm // n_sub

    def axial(gain, scale):
        def f(a, rows):
            r = _rotate(_rms(a, gain), ca_ref[rows, :], sa_ref[rows, :], HEAD_DIM // 4)
            return r * scale if scale != 1.0 else r
        return f

    def partial_rot(scale):
        def f(a, rows):
            r = _rotate(a, cb_ref[rows, :], sb_ref[rows, :], ROT_DIM_B // 2)
            return r * scale if scale != 1.0 else r
        return f

    def plain(a, rows):
        return a

    def gate(a, rows):
        return jax.nn.sigmoid(a)

    def run(epilogues):
        for r in range(n_sub):
            rows = slice(r * sub, (r + 1) * sub)
            acc = jnp.dot(xn_ref[rows, :], w_ref[...], preferred_element_type=F32)
            for c, fn in enumerate(epilogues):
                cols = slice(c * LANES, (c + 1) * LANES)
                o_ref[rows, cols] = fn(acc[:, cols], rows).astype(o_ref.dtype)

    n_chunks = PROJ_TN // LANES
    q_gain, k_gain = qk_ref[0:1, :], qk_ref[1:2, :]

    @pl.when(j < 8)
    def _():
        run([gate] * n_chunks)

    @pl.when((j >= 8) & (j < 10))
    def _():
        run([axial(q_gain, QK_SCALE_LOG2E)] * n_chunks)

    @pl.when(j == 10)
    def _():
        run([axial(k_gain, 1.0), axial(k_gain, 1.0), plain, plain])

    @pl.when((j >= 11) & (j < 13))
    def _():
        run([partial_rot(QK_SCALE_LOG2E)] * n_chunks)

    @pl.when((j >= 13) & (j < 15))
    def _():
        run([partial_rot(1.0)] * n_chunks)

    @pl.when(j >= 15)
    def _():
        run([plain] * n_chunks)


def _rope_tables(seq):
    t = jnp.arange(seq, dtype=jnp.int32)
    row = (t // GRID_W).astype(F32)
    col = (t % GRID_W).astype(F32)
    lane = jnp.arange(LANES, dtype=jnp.int32)
    half = HEAD_DIM // 4
    inv_a = jnp.power(AXIAL_THETA, -2.0 * (lane % half).astype(F32) / (HEAD_DIM // 2))
    pos_a = jnp.where(lane[None, :] < HEAD_DIM // 2, row[:, None], col[:, None])
    ang_a = pos_a * inv_a[None, :]
    sign_a = jnp.where((lane % (2 * half)) < half, -1.0, 1.0).astype(F32)
    ca, sa = jnp.cos(ang_a), jnp.sin(ang_a) * sign_a[None, :]
    hb = ROT_DIM_B // 2
    inv_b = jnp.power(ROPE_THETA, -2.0 * (lane % hb).astype(F32) / ROT_DIM_B)
    ang_b = t.astype(F32)[:, None] * inv_b[None, :]
    rot = (lane < ROT_DIM_B)[None, :]
    sign_b = jnp.where((lane % (2 * hb)) < hb, -1.0, 1.0).astype(F32)
    cb = jnp.where(rot, jnp.cos(ang_b), 1.0)
    sb = jnp.where(rot, jnp.sin(ang_b) * sign_b[None, :], 0.0)
    return ca, sa, cb, sb


def _proj(x, g, w, tables, qk_gain, seq, tm=1024):
    T = x.shape[0]
    tm = min(tm, seq)
    nseq = seq // tm
    tab_spec = pl.BlockSpec((tm, LANES), lambda i, j: (i % nseq, 0))
    return pl.pallas_call(
        _proj_kernel,
        grid=(T // tm, IN_COLS // PROJ_TN),
        in_specs=[
            pl.BlockSpec((tm, D_MODEL), lambda i, j: (i, 0)),
            pl.BlockSpec((1, D_MODEL), lambda i, j: (0, 0)),
            pl.BlockSpec((D_MODEL, PROJ_TN), lambda i, j: (0, j)),
            tab_spec, tab_spec, tab_spec, tab_spec,
            pl.BlockSpec((2, LANES), lambda i, j: (0, 0)),
        ],
        out_specs=pl.BlockSpec((tm, PROJ_TN), lambda i, j: (i, j)),
        out_shape=jax.ShapeDtypeStruct((T, IN_COLS), BF16),
        scratch_shapes=[pltpu.VMEM((tm, D_MODEL), BF16)],
        compiler_params=_cparams(("parallel", "arbitrary")),
        name="proj",
    )(x, g, w, *tables, qk_gain)


def _nt_dot(a, b):
    return lax.dot_general(a, b, (((1,), (1,)), ((), ())), preferred_element_type=F32)


def _lane_tile(x, n):
    return jnp.concatenate([x] * n, axis=1)


def _softmax_block(s, m_ref):
    m_prev = m_ref[...]
    m_new = jnp.maximum(m_prev, jnp.max(s, axis=-1, keepdims=True))
    alpha = jnp.exp2(m_prev - m_new)
    p = jnp.exp2(s - _lane_tile(m_new, s.shape[1] // LANES))
    m_ref[...] = m_new
    return alpha, p


def _attn_a_kernel(q_ref, k_ref, v_ref, o_ref, m_ref, acc_ref, vaug_ref, *, tk, unroll):
    tq = q_ref.shape[0]
    group = N_HEADS_A // N_KV_A

    @pl.when(pl.program_id(2) == 0)
    def _():
        vaug_ref[:, :HEAD_DIM] = v_ref[...]
        vaug_ref[:, HEAD_DIM:] = jnp.ones(v_ref.shape, BF16)

    m_ref[...] = jnp.full(m_ref.shape, NEG_BIG, F32)
    acc_ref[...] = jnp.zeros(acc_ref.shape, F32)

    def body(i, carry):
        for u in range(unroll):
            off = pl.multiple_of((i * unroll + u) * tk, tk)
            k = k_ref[pl.ds(off, tk), :]
            v = vaug_ref[pl.ds(off, tk), :]
            for g in range(group):
                rows = pl.ds(g * tq, tq)
                s = _nt_dot(q_ref[:, g * HEAD_DIM:(g + 1) * HEAD_DIM], k)
                alpha, p = _softmax_block(s, m_ref.at[rows])
                pv = jnp.dot(p.astype(BF16), v, preferred_element_type=F32)
                acc_ref[rows, :] = _lane_tile(alpha, 2) * acc_ref[rows, :] + pv
        return carry

    lax.fori_loop(0, k_ref.shape[0] // (tk * unroll), body, 0)
    o = acc_ref[:, :HEAD_DIM] / acc_ref[:, HEAD_DIM:]
    for g in range(group):
        o_ref[:, g * HEAD_DIM:(g + 1) * HEAD_DIM] = o[g * tq:(g + 1) * tq].astype(o_ref.dtype)


def _single_buffered(shape, index_map):
    return pl.BlockSpec(shape, index_map, pipeline_mode=pl.Buffered(1))


def _attn_a(p, batch, seq, tq=512, tk=2048):
    T = p.shape[0]
    group = N_HEADS_A // N_KV_A
    qw = group * HEAD_DIM
    tq = min(tq, seq)
    tk = min(tk, seq)
    nq = seq // tq
    unroll = 2 if (seq // tk) % 2 == 0 else 1
    return pl.pallas_call(
        functools.partial(_attn_a_kernel, tk=tk, unroll=unroll),
        grid=(batch, N_KV_A, nq),
        in_specs=[
            pl.BlockSpec((tq, qw), lambda b, h, i: (b * nq + i, COL_QA // qw + h)),
            _single_buffered((seq, HEAD_DIM), lambda b, h, i: (b, COL_KA // HEAD_DIM + h)),
            _single_buffered((seq, HEAD_DIM), lambda b, h, i: (b, COL_VA // HEAD_DIM + h)),
        ],
        out_specs=pl.BlockSpec((tq, qw), lambda b, h, i: (b * nq + i, h)),
        out_shape=jax.ShapeDtypeStruct((T, N_HEADS_A * HEAD_DIM), BF16),
        scratch_shapes=[pltpu.VMEM((group * tq, LANES), F32),
                        pltpu.VMEM((group * tq, 2 * HEAD_DIM), F32),
                        pltpu.VMEM((seq, 2 * HEAD_DIM), BF16)],
        compiler_params=_cparams(("parallel", "parallel", "arbitrary")),
        name="attn_a",
    )(p, p, p)


def _attn_b_kernel(lam_ref, q_ref, k_ref, v_ref, sub_ref, o_ref, m_ref, l_ref, acc_ref, *, tk, sub, unroll, lam_init):
    tq = q_ref.shape[0]
    m_ref[...] = jnp.full(m_ref.shape, NEG_BIG, F32)
    l_ref[...] = jnp.zeros(l_ref.shape, F32)
    acc_ref[...] = jnp.zeros(acc_ref.shape, F32)

    def body(i, carry):
        for u in range(unroll):
            off = pl.multiple_of((i * unroll + u) * tk, tk)
            v = v_ref[pl.ds(off, tk), :]
            for c in range(2):
                cols = slice(c * HEAD_DIM, (c + 1) * HEAD_DIM)
                k = k_ref[pl.ds(off, tk), cols]
                for r in range(tq // sub):
                    rows = pl.ds(c * tq + r * sub, sub)
                    s = _nt_dot(q_ref[r * sub:(r + 1) * sub, cols], k)
                    alpha, p = _softmax_block(s, m_ref.at[rows])
                    l_ref[rows, :] = alpha * l_ref[rows, :] + jnp.sum(p, axis=-1, keepdims=True)
                    pv = jnp.dot(p.astype(BF16), v, preferred_element_type=F32)
                    acc_ref[rows, :] = _lane_tile(alpha, 2) * acc_ref[rows, :] + pv
        return carry

    lax.fori_loop(0, k_ref.shape[0] // (tk * unroll), body, 0)
    lv = lam_ref[...]
    lam = (jnp.exp(jnp.sum(lv[0:1] * lv[1:2], axis=-1, keepdims=True))
           - jnp.exp(jnp.sum(lv[2:3] * lv[3:4], axis=-1, keepdims=True)) + lam_init)
    o = acc_ref[...] / _lane_tile(l_ref[...], 2)
    od = o[:tq] - lam * o[tq:]
    o_ref[...] = (_rms(od, sub_ref[...]) * (1.0 - lam_init)).astype(o_ref.dtype)


def _attn_b(p, lam_vecs, subln, batch, seq, lam_init, tq=1024, sub=512, tk=2048):
    T = p.shape[0]
    vw = 2 * HEAD_DIM
    tq = min(tq, seq)
    sub = min(sub, tq)
    tk = min(tk, seq)
    nq = seq // tq
    unroll = 2 if (seq // tk) % 2 == 0 else 1
    return pl.pallas_call(
        functools.partial(_attn_b_kernel, tk=tk, sub=sub, unroll=unroll, lam_init=lam_init),
        grid=(batch, N_HEADS_B, nq),
        in_specs=[
            pl.BlockSpec((4, HEAD_DIM), lambda b, h, i: (0, 0)),
            pl.BlockSpec((tq, vw), lambda b, h, i: (b * nq + i, COL_QB // vw + h)),
            _single_buffered((seq, vw), lambda b, h, i: (b, COL_KB // vw + h)),
            _single_buffered((seq, vw), lambda b, h, i: (b, COL_VB // vw + h)),
            pl.BlockSpec((1, vw), lambda b, h, i: (0, 0)),
        ],
        out_specs=pl.BlockSpec((tq, vw), lambda b, h, i: (b * nq + i, h)),
        out_shape=jax.ShapeDtypeStruct((T, N_HEADS_B * vw), BF16),
        scratch_shapes=[pltpu.VMEM((2 * tq, LANES), F32), pltpu.VMEM((2 * tq, LANES), F32),
                        pltpu.VMEM((2 * tq, vw), F32)],
        compiler_params=_cparams(("parallel", "parallel", "parallel")),
        name="attn_b",
    )(lam_vecs, p, p, p, subln)


def _merge_kernel(x_ref, oa_ref, ob_ref, ga_ref, gb_ref, wa_ref, wb_ref, wo_ref, o_ref):
    a = jnp.dot(oa_ref[...], wa_ref[...], preferred_element_type=F32)
    b = jnp.dot(ob_ref[...], wb_ref[...], preferred_element_type=F32)
    merged = ga_ref[...].astype(F32) * a + gb_ref[...].astype(F32) * b
    o_ref[...] = x_ref[...] + jnp.dot(merged.astype(BF16), wo_ref[...], preferred_element_type=F32)


def _resident(shape):
    return pl.BlockSpec(shape, lambda *_: (0,) * len(shape), pipeline_mode=pl.Buffered(1))


def _merge(x, oa, ob, p, wa, wb, wo, tm=256):
    T = x.shape[0]
    return pl.pallas_call(
        _merge_kernel,
        grid=(T // tm,),
        in_specs=[
            pl.BlockSpec((tm, D_MODEL), lambda i: (i, 0)),
            pl.BlockSpec((tm, oa.shape[1]), lambda i: (i, 0)),
            pl.BlockSpec((tm, ob.shape[1]), lambda i: (i, 0)),
            pl.BlockSpec((tm, D_MODEL), lambda i: (i, COL_GA // D_MODEL)),
            pl.BlockSpec((tm, D_MODEL), lambda i: (i, COL_GB // D_MODEL)),
            _resident(wa.shape), _resident(wb.shape), _resident(wo.shape),
        ],
        out_specs=pl.BlockSpec((tm, D_MODEL), lambda i: (i, 0)),
        out_shape=jax.ShapeDtypeStruct((T, D_MODEL), F32),
        compiler_params=_cparams(("parallel",)),
        name="merge",
    )(x, oa, ob, p, p, wa, wb, wo)


def _norm_matmul_kernel(x_ref, g_ref, w_ref, o_ref):
    xn = _rms(x_ref[...], g_ref[...]).astype(BF16)
    o_ref[...] = jnp.dot(xn, w_ref[...], preferred_element_type=F32).astype(o_ref.dtype)


def _norm_matmul(x, g, w, tm=256):
    T, K = x.shape
    N = w.shape[1]
    return pl.pallas_call(
        _norm_matmul_kernel,
        grid=(T // tm,),
        in_specs=[pl.BlockSpec((tm, K), lambda i: (i, 0)), pl.BlockSpec((1, K), lambda i: (0, 0)),
                  pl.BlockSpec((K, N), lambda i: (0, 0))],
        out_specs=pl.BlockSpec((tm, N), lambda i: (i, 0)),
        out_shape=jax.ShapeDtypeStruct((T, N), BF16),
        compiler_params=_cparams(("parallel",)),
        name="mem_kv",
    )(x, g, w)


def _cross_kernel(x_ref, g_ref, wq_ref, kv_ref, wo_ref, o_ref):
    x = x_ref[...]
    hn = _rms(x, g_ref[...]).astype(BF16)
    q = (jnp.dot(hn, wq_ref[...], preferred_element_type=F32) * QK_SCALE).astype(BF16)
    heads = []
    for h in range(N_HEADS_X):
        lo = h * HEAD_DIM
        s = _nt_dot(q[:, lo:lo + HEAD_DIM], kv_ref[:, lo:lo + HEAD_DIM])
        e = jnp.exp(s - jnp.max(s, axis=-1, keepdims=True))
        pr = (e / jnp.sum(e, axis=-1, keepdims=True)).astype(BF16)
        heads.append(jnp.dot(pr, kv_ref[:, WIDTH_X + lo:WIDTH_X + lo + HEAD_DIM], preferred_element_type=F32))
    o = jnp.concatenate(heads, axis=1).astype(BF16)
    o_ref[...] = x + jnp.dot(o, wo_ref[...], preferred_element_type=F32)


def _cross(x, g, wq, kv, wo, seq, tm=512):
    T = x.shape[0]
    tm = min(tm, seq)
    per_seq = seq // tm
    return pl.pallas_call(
        _cross_kernel,
        grid=(T // tm,),
        in_specs=[
            pl.BlockSpec((tm, D_MODEL), lambda i: (i, 0)),
            pl.BlockSpec((1, D_MODEL), lambda i: (0, 0)),
            pl.BlockSpec(wq.shape, lambda i: (0, 0)),
            pl.BlockSpec((MEM_LEN, 2 * WIDTH_X), lambda i: (i // per_seq, 0)),
            pl.BlockSpec(wo.shape, lambda i: (0, 0)),
        ],
        out_specs=pl.BlockSpec((tm, D_MODEL), lambda i: (i, 0)),
        out_shape=jax.ShapeDtypeStruct((T, D_MODEL), F32),
        compiler_params=_cparams(("parallel",)),
        name="cross",
    )(x, g, wq, kv, wo)


def _router_kernel(x_ref, g_ref, wr_ref, br_ref, cin_ref, hn_ref, ids_ref, wts_ref, rank_ref, cnt_ref, run_ref):
    i = pl.program_id(0)

    @pl.when(i == 0)
    def _():
        run_ref[...] = cin_ref[...]

    hn = _rms(x_ref[...], g_ref[...])
    hn_ref[...] = hn
    logits = jnp.dot(hn, wr_ref[...], preferred_element_type=F32, precision=lax.Precision.HIGHEST) + br_ref[...]
    tm = logits.shape[0]
    lane = lax.broadcasted_iota(jnp.int32, logits.shape, 1)
    lane_f = lane.astype(F32)
    vals = logits
    top_v, top_i, sels = [], [], []
    for _ in range(TOP_K):
        mk = jnp.max(vals, axis=-1, keepdims=True)
        idx = jnp.min(jnp.where(vals == mk, lane_f, float(LANES)), axis=-1, keepdims=True)
        sel = lane_f == idx
        top_v.append(mk)
        top_i.append(idx.astype(jnp.int32))
        sels.append(sel)
        vals = jnp.where(sel, 2.0 * NEG_BIG, vals)
    es = [jnp.exp(v - top_v[0]) for v in top_v]
    denom = es[0] + es[1] + es[2] + es[3]
    onehot = (sels[0] | sels[1] | sels[2] | sels[3]).astype(F32)
    r = lax.broadcasted_iota(jnp.int32, (tm, tm), 0)
    c = lax.broadcasted_iota(jnp.int32, (tm, tm), 1)
    tri = (c < r).astype(BF16)
    prefix = jnp.dot(tri, onehot.astype(BF16), preferred_element_type=F32)
    rank_te = prefix + run_ref[0:1, :]
    ids = jnp.zeros(logits.shape, jnp.int32)
    wts = jnp.zeros(logits.shape, F32)
    rank = jnp.zeros(logits.shape, jnp.int32)
    for k in range(TOP_K):
        rk = jnp.sum(jnp.where(sels[k], rank_te, 0.0), axis=-1, keepdims=True).astype(jnp.int32)
        ids = jnp.where(lane == k, top_i[k], ids)
        wts = jnp.where(lane == k, es[k] / denom, wts)
        rank = jnp.where(lane == k, rk, rank)
    ids_ref[...] = ids
    wts_ref[...] = wts
    rank_ref[...] = rank
    run_ref[...] = run_ref[...] + jnp.sum(onehot, axis=0, keepdims=True)
    cnt_ref[...] = run_ref[...]


def _router(x, g, wr, br, counts_in, tm=512):
    T = x.shape[0]
    row = pl.BlockSpec((tm, LANES), lambda i: (i, 0))
    return pl.pallas_call(
        _router_kernel,
        grid=(T // tm,),
        in_specs=[pl.BlockSpec((tm, D_MODEL), lambda i: (i, 0)), pl.BlockSpec((1, D_MODEL), lambda i: (0, 0)),
                  pl.BlockSpec((D_MODEL, LANES), lambda i: (0, 0)), pl.BlockSpec((1, LANES), lambda i: (0, 0)),
                  pl.BlockSpec((8, LANES), lambda i: (0, 0))],
        out_specs=[pl.BlockSpec((tm, D_MODEL), lambda i: (i, 0)), row, row, row,
                   pl.BlockSpec((8, LANES), lambda i: (0, 0))],
        out_shape=[jax.ShapeDtypeStruct((T, D_MODEL), F32), jax.ShapeDtypeStruct((T, LANES), jnp.int32),
                   jax.ShapeDtypeStruct((T, LANES), F32), jax.ShapeDtypeStruct((T, LANES), jnp.int32),
                   jax.ShapeDtypeStruct((8, LANES), F32)],
        scratch_shapes=[pltpu.VMEM((8, LANES), F32)],
        compiler_params=_cparams(("arbitrary",)),
        name="router",
    )(x, g, wr, br, counts_in)


DISPATCH_CHUNK = 512


def _dispatch_kernel(gs_ref, ids_ref, rank_ref, hn_ref, xs_in_ref, xs_ref, sem):
    del xs_in_ref

    def row_copy(src_row, dst_row):
        return pltpu.make_async_copy(hn_ref.at[pl.ds(src_row, 1)], xs_ref.at[pl.ds(dst_row, 1)], sem)

    def issue(c, carry):
        for k in range(TOP_K):
            slot = gs_ref[ids_ref[0, 0, c * TOP_K + k]] + rank_ref[0, 0, c * TOP_K + k]
            row_copy(c, slot).start()
        return carry

    lax.fori_loop(0, DISPATCH_CHUNK, issue, 0)
    for k in range(TOP_K):
        pltpu.make_async_copy(hn_ref, xs_ref.at[pl.ds(0, DISPATCH_CHUNK)], sem).wait()


def _dispatch(hn, ids4, rank4, group_start, xs0):
    T = hn.shape[0]
    n = T // DISPATCH_CHUNK
    n_slots = xs0.shape[0]
    idx_spec = pl.BlockSpec((1, 1, DISPATCH_CHUNK * TOP_K), lambda i, gs: (i, 0, 0), memory_space=pltpu.SMEM)
    return pl.pallas_call(
        _dispatch_kernel,
        grid_spec=pltpu.PrefetchScalarGridSpec(
            num_scalar_prefetch=1,
            grid=(n,),
            in_specs=[idx_spec, idx_spec, pl.BlockSpec((DISPATCH_CHUNK, D_MODEL), lambda i, gs: (i, 0)),
                      pl.BlockSpec(memory_space=pl.ANY)],
            out_specs=pl.BlockSpec(memory_space=pl.ANY),
            scratch_shapes=[pltpu.SemaphoreType.DMA(())],
        ),
        out_shape=jax.ShapeDtypeStruct((n_slots, D_MODEL), F32),
        input_output_aliases={4: 0},
        compiler_params=_cparams(("arbitrary",)),
        name="dispatch",
    )(group_start, ids4.reshape(n, 1, -1), rank4.reshape(n, 1, -1), hn, xs0)


SPLIT_TN = 512


def _split_glu_kernel(w_ref, perm_ref, g_ref, l_ref):
    r = jnp.dot(w_ref[0].astype(BF16), perm_ref[...], preferred_element_type=F32)
    half = SPLIT_TN // 2
    g_ref[0] = r[:, :half].astype(BF16)
    l_ref[0] = r[:, half:].astype(BF16)


def _split_glu(w1):
    E, K, N2 = w1.shape
    half = SPLIT_TN // 2
    src = jnp.arange(SPLIT_TN, dtype=jnp.int32)
    dst = jnp.where(src % 2 == 0, src // 2, half + src // 2)
    perm = (dst[:, None] == jnp.arange(SPLIT_TN, dtype=jnp.int32)[None, :]).astype(BF16)
    out = jax.ShapeDtypeStruct((E, K, N2 // 2), BF16)
    return pl.pallas_call(
        _split_glu_kernel,
        grid=(E, N2 // SPLIT_TN),
        in_specs=[pl.BlockSpec((1, K, SPLIT_TN), lambda e, j: (e, 0, j)),
                  pl.BlockSpec((SPLIT_TN, SPLIT_TN), lambda e, j: (0, 0))],
        out_specs=[pl.BlockSpec((1, K, half), lambda e, j: (e, 0, j)),
                   pl.BlockSpec((1, K, half), lambda e, j: (e, 0, j))],
        out_shape=[out, out],
        compiler_params=_cparams(("parallel", "parallel")),
        name="split_glu",
    )(w1, perm)


MOE_TM = 512
MOE_TF = 512


MOE_GROUP = 2


def _experts_kernel(ge_ref, tv_ref, xs_ref, w1g_ref, w1l_ref, b1g_ref, b1l_ref, w2_ref, b2_ref, ys_ref, xb_ref, acc_ref):
    i = pl.program_id(0)
    f = pl.program_id(1)
    s = pl.program_id(2)
    last = pl.num_programs(1) - 1
    valid = tv_ref[i * MOE_GROUP + s] > 0

    @pl.when(valid)
    def _():
        @pl.when(f == 0)
        def _():
            xb_ref[s] = xs_ref[...].astype(BF16)

        xb = xb_ref[s]
        glu = jnp.dot(xb, w1g_ref[0], preferred_element_type=F32) + b1g_ref[0]
        lin = jnp.dot(xb, w1l_ref[0], preferred_element_type=F32) + b1l_ref[0]
        glu = jnp.minimum(glu, SWIGLU_LIMIT)
        lin = jnp.clip(lin, -SWIGLU_LIMIT, SWIGLU_LIMIT)
        act = glu * jax.nn.sigmoid(SWIGLU_ALPHA * glu) * (lin + 1.0)
        part = jnp.dot(act.astype(BF16), w2_ref[0], preferred_element_type=F32)

        @pl.when(f == 0)
        def _():
            acc_ref[s] = part

        @pl.when((f > 0) & (f < last))
        def _():
            acc_ref[s] += part

        @pl.when(f == last)
        def _():
            ys_ref[...] = acc_ref[s] + part + b2_ref[0]

    @pl.when(jnp.logical_not(valid) & (f == last))
    def _():
        ys_ref[...] = jnp.zeros(ys_ref.shape, F32)


def _experts(xs, group_expert, tile_valid, w1g, w1l, b1g, b1l, w2, b2):
    n_slots = xs.shape[0]
    n_groups = n_slots // (MOE_TM * MOE_GROUP)
    nf = D_FF // MOE_TF
    assert nf >= 2
    G = MOE_GROUP

    def xs_map(i, f, s, ge, tv):
        return (jnp.where(f == 0, i * G + s, i * G + G - 1), 0)

    def ys_map(i, f, s, ge, tv):
        return (jnp.where(f == nf - 1, i * G + s, i * G), 0)

    return pl.pallas_call(
        _experts_kernel,
        grid_spec=pltpu.PrefetchScalarGridSpec(
            num_scalar_prefetch=2,
            grid=(n_groups, nf, G),
            in_specs=[
                pl.BlockSpec((MOE_TM, D_MODEL), xs_map),
                pl.BlockSpec((1, D_MODEL, MOE_TF), lambda i, f, s, ge, tv: (ge[i], 0, f)),
                pl.BlockSpec((1, D_MODEL, MOE_TF), lambda i, f, s, ge, tv: (ge[i], 0, f)),
                pl.BlockSpec((1, 1, MOE_TF), lambda i, f, s, ge, tv: (ge[i], 0, f)),
                pl.BlockSpec((1, 1, MOE_TF), lambda i, f, s, ge, tv: (ge[i], 0, f)),
                pl.BlockSpec((1, MOE_TF, D_MODEL), lambda i, f, s, ge, tv: (ge[i], f, 0)),
                pl.BlockSpec((1, 1, D_MODEL), lambda i, f, s, ge, tv: (ge[i], 0, 0)),
            ],
            out_specs=pl.BlockSpec((MOE_TM, D_MODEL), ys_map),
            scratch_shapes=[pltpu.VMEM((G, MOE_TM, D_MODEL), BF16), pltpu.VMEM((G, MOE_TM, D_MODEL), F32)],
        ),
        out_shape=jax.ShapeDtypeStruct((n_slots, D_MODEL), F32),
        compiler_params=_cparams(("parallel", "arbitrary", "arbitrary")),
        name="experts",
    )(group_expert, tile_valid, xs, w1g, w1l, b1g, b1l, w2, b2)


COMBINE_CHUNK = 128


def _combine_kernel(gs_ref, ids_ref, rank_ref, ids_nxt_ref, rank_nxt_ref, x_ref, wts_ref, g_ref, ys_ref, o_ref,
                    buf_ref, sem):
    i = pl.program_id(0)
    n = pl.num_programs(0)
    cur = i % 2

    def gather(ids, rank, buf):
        def issue(c, carry):
            for k in range(TOP_K):
                slot = gs_ref[ids[0, 0, c * TOP_K + k]] + rank[0, 0, c * TOP_K + k]
                pltpu.make_async_copy(ys_ref.at[pl.ds(slot, 1)], buf_ref.at[buf, k, pl.ds(c, 1)], sem.at[buf]).start()
            return carry
        lax.fori_loop(0, COMBINE_CHUNK, issue, 0)

    @pl.when(i == 0)
    def _():
        gather(ids_ref, rank_ref, 0)

    @pl.when(i + 1 < n)
    def _():
        gather(ids_nxt_ref, rank_nxt_ref, 1 - cur)

    for k in range(TOP_K):
        pltpu.make_async_copy(ys_ref.at[pl.ds(0, COMBINE_CHUNK)], buf_ref.at[cur, k], sem.at[cur]).wait()
    wts = wts_ref[...]
    y = x_ref[...]
    for k in range(TOP_K):
        y = y + wts[:, k:k + 1] * buf_ref[cur, k]
    o_ref[...] = _rms(y, g_ref[...])


def _combine(x, wts, g, ys, ids4, rank4, group_start):
    T = x.shape[0]
    n = T // COMBINE_CHUNK
    idx_shape = (1, 1, COMBINE_CHUNK * TOP_K)
    idx_spec = pl.BlockSpec(idx_shape, lambda i, gs: (i, 0, 0), memory_space=pltpu.SMEM)
    nxt_spec = pl.BlockSpec(idx_shape, lambda i, gs: (jnp.minimum(i + 1, n - 1), 0, 0), memory_space=pltpu.SMEM)
    ids_r, rank_r = ids4.reshape(n, 1, -1), rank4.reshape(n, 1, -1)
    return pl.pallas_call(
        _combine_kernel,
        grid_spec=pltpu.PrefetchScalarGridSpec(
            num_scalar_prefetch=1,
            grid=(n,),
            in_specs=[idx_spec, idx_spec, nxt_spec, nxt_spec,
                      pl.BlockSpec((COMBINE_CHUNK, D_MODEL), lambda i, gs: (i, 0)),
                      pl.BlockSpec((COMBINE_CHUNK, LANES), lambda i, gs: (i, 0)),
                      pl.BlockSpec((1, D_MODEL), lambda i, gs: (0, 0)),
                      pl.BlockSpec(memory_space=pl.ANY)],
            out_specs=pl.BlockSpec((COMBINE_CHUNK, D_MODEL), lambda i, gs: (i, 0)),
            scratch_shapes=[pltpu.VMEM((2, TOP_K, COMBINE_CHUNK, D_MODEL), F32), pltpu.SemaphoreType.DMA((2,))],
        ),
        out_shape=jax.ShapeDtypeStruct((T, D_MODEL), F32),
        compiler_params=_cparams(("arbitrary",)),
        name="combine",
    )(group_start, ids_r, rank_r, ids_r, rank_r, x, wts, g, ys)


def _mix_and_cross(x3, mem3, w, lam_init):
    batch, seq, _ = x3.shape
    x = x3.reshape(batch * seq, D_MODEL)
    mem = mem3.reshape(batch * MEM_LEN, D_MODEL)
    p = _proj(x, w["norm_mix"], w["w_in"], _rope_tables(seq), w["qk_gain"], seq)
    oa = _attn_a(p, batch, seq)
    ob = _attn_b(p, w["lam_vecs"], w["subln_b"], batch, seq, lam_init)
    x = _merge(x, oa, ob, p, w["w_branch_a"], w["w_branch_b"], w["w_out"])
    kv = _norm_matmul(mem, w["norm_mem"], w["w_ckv"])
    return _cross(x, w["norm_cross"], w["w_cq"], kv, w["w_co"], seq)


def _moe_and_final_norm(xs_list, w):
    counts = jnp.zeros((8, LANES), F32)
    routed = []
    for x in xs_list:
        hn, ids, gate_w, rank, counts = _router(x, w["norm_moe"], w["w_router"], w["b_router"], counts)
        routed.append((hn, ids[:, :TOP_K], rank[:, :TOP_K], gate_w))
    rows_per_group = MOE_TM * MOE_GROUP
    cnt = counts[0, :N_EXPERTS].astype(jnp.int32)
    padded = ((cnt + rows_per_group - 1) // rows_per_group) * rows_per_group
    group_end = jnp.cumsum(padded)
    group_start = group_end - padded
    total = sum(x.shape[0] for x in xs_list) * TOP_K
    n_groups = total // rows_per_group + N_EXPERTS
    lo = jnp.arange(n_groups, dtype=jnp.int32) * rows_per_group
    group_expert = jnp.sum((lo[:, None] >= group_end[None, :]).astype(jnp.int32), axis=1)
    last_expert = jnp.max(jnp.where(cnt > 0, jnp.arange(N_EXPERTS, dtype=jnp.int32), 0))
    group_expert = jnp.minimum(group_expert, last_expert).astype(jnp.int32)
    tile_lo = jnp.arange(n_groups * MOE_GROUP, dtype=jnp.int32) * MOE_TM
    tile_e = jnp.repeat(group_expert, MOE_GROUP)
    tile_valid = (tile_lo < group_start[tile_e] + cnt[tile_e]).astype(jnp.int32)
    slots = jnp.zeros((n_groups * rows_per_group, D_MODEL), F32)
    for hn, ids4, rank4, _ in routed:
        slots = _dispatch(hn, ids4, rank4, group_start, slots)
    ys = _experts(slots, group_expert, tile_valid, w["w1g"], w["w1l"], w["b1g"], w["b1l"], w["w2"], w["b2"])
    return [_combine(x, gate_w, w["norm_final"], ys, ids4, rank4, group_start)
            for x, (hn, ids4, rank4, gate_w) in zip(xs_list, routed)]


def kernel(x_prompt, x_sample, mem_prompt, mem_sample, norm_mix, w_in, q_norm_a, k_norm_a, lambda_q1, lambda_k1, lambda_q2, lambda_k2, subln_b, w_branch_a, w_branch_b, w_out, norm_cross, norm_mem, w_cq, w_ckv, w_co, norm_moe, w_router, b_router, w_exp1, b_exp1, w_exp2, b_exp2, norm_final):
    assert DEPTH == 1
    wts = []
    for l in range(DEPTH):
        wr = jnp.zeros((D_MODEL, LANES), F32).at[:, :N_EXPERTS].set(w_router[l])
        br = jnp.full((1, LANES), NEG_BIG, F32).at[0, :N_EXPERTS].set(b_router[l])
        w1g, w1l = _split_glu(w_exp1[l])
        wts.append(dict(
            norm_mix=norm_mix[l][None, :],
            w_in=jnp.concatenate([w_in[l][:, REF_GATE_START:], w_in[l][:, :REF_GATE_START]], axis=1).astype(BF16),
            qk_gain=jnp.stack([q_norm_a[l], k_norm_a[l]]),
            lam_vecs=jnp.stack([lambda_q1[l], lambda_k1[l], lambda_q2[l], lambda_k2[l]]),
            subln_b=subln_b[l][None, :],
            w_branch_a=w_branch_a[l].astype(BF16),
            w_branch_b=w_branch_b[l].astype(BF16),
            w_out=w_out[l].astype(BF16),
            norm_cross=norm_cross[l][None, :],
            norm_mem=norm_mem[l][None, :],
            w_cq=w_cq[l].astype(BF16),
            w_ckv=w_ckv[l].astype(BF16),
            w_co=w_co[l].astype(BF16),
            norm_moe=norm_moe[l][None, :],
            w_router=wr,
            b_router=br,
            w1g=w1g,
            w1l=w1l,
            b1g=b_exp1[l][:, None, 0::2],
            b1l=b_exp1[l][:, None, 1::2],
            w2=w_exp2[l].astype(BF16),
            b2=b_exp2[l][:, None, :],
            norm_final=norm_final[None, :],
        ))
    w = wts[0]
    lam_init = 0.8 - 0.6 * math.exp(-0.3 * 0)
    streams = [_mix_and_cross(x_prompt, mem_prompt, w, lam_init), _mix_and_cross(x_sample, mem_sample, w, lam_init)]
    y_prompt, y_sample = _moe_and_final_norm(streams, w)
    return (y_prompt.reshape(x_prompt.shape), y_sample.reshape(x_sample.shape))
```

```python
import functools
import math

import jax
import jax.numpy as jnp
from jax import lax
from jax.experimental import pallas as pl
from jax.experimental.pallas import tpu as pltpu

F32 = jnp.float32
BF16 = jnp.bfloat16

D_MODEL = 2048
DEPTH = 1
GRID_W = 64
RMS_EPS = 1e-6
N_HEADS_A = 8
N_KV_A = 2
HEAD_DIM = 128
AXIAL_THETA = 10000.0
N_HEADS_B = 4
ROPE_THETA = 500000.0
ROT_DIM_B = HEAD_DIM // 4
MEM_LEN = 256
N_HEADS_X = 4
WIDTH_X = N_HEADS_X * HEAD_DIM
N_EXPERTS = 32
TOP_K = 4
D_FF = D_MODEL
SWIGLU_ALPHA = 1.702
SWIGLU_LIMIT = 7.0
LANES = 128
IN_COLS = 8704
QK_SCALE = HEAD_DIM ** -0.5
QK_SCALE_LOG2E = QK_SCALE * math.log2(math.e)
NEG_BIG = -1e30

VMEM_LIMIT = 56 * 1024 * 1024

COL_GA = 0
COL_GB = 2048
COL_QA = 4096
COL_KA = 5120
COL_VA = 5376
COL_QB = 5632
COL_KB = 6656
COL_VB = 7680
REF_GATE_START = 4608


def _cparams(sem):
    return pltpu.CompilerParams(dimension_semantics=sem, vmem_limit_bytes=VMEM_LIMIT)


def _rms(x, g):
    return x * lax.rsqrt(jnp.mean(x * x, axis=-1, keepdims=True) + RMS_EPS) * g


def _rotate(x, cos, sin_signed, shift):
    n = x.shape[-1]
    lane = lax.broadcasted_iota(jnp.int32, x.shape, 1)
    up = pltpu.roll(x, n - shift, 1)
    dn = pltpu.roll(x, shift, 1)
    partner = jnp.where((lane % (2 * shift)) < shift, up, dn)
    return x * cos + partner * sin_signed


PROJ_TN = 512


def _proj_kernel(x_ref, g_ref, w_ref, ca_ref, sa_ref, cb_ref, sb_ref, qk_ref, o_ref, xn_ref):
    j = pl.program_id(1)

    @pl.when(j == 0)
    def _():
        xn_ref[...] = _rms(x_ref[...], g_ref[...]).astype(BF16)

    tm = xn_ref.shape[0]
    n_sub = 2 if tm % 16 == 0 else 1
    sub = tm // n_sub

    def axial(gain, scale):
        def f(a, rows):
            r = _rotate(_rms(a, gain), ca_ref[rows, :], sa_ref[rows, :], HEAD_DIM // 4)
            return r * scale if scale != 1.0 else r
        return f

    def partial_rot(scale):
        def f(a, rows):
            r = _rotate(a, cb_ref[rows, :], sb_ref[rows, :], ROT_DIM_B // 2)
            return r * scale if scale != 1.0 else r
        return f

    def plain(a, rows):
        return a

    def gate(a, rows):
        return jax.nn.sigmoid(a)

    def run(epilogues):
        for r in range(n_sub):
            rows = slice(r * sub, (r + 1) * sub)
            acc = jnp.dot(xn_ref[rows, :], w_ref[...], preferred_element_type=F32)
            for c, fn in enumerate(epilogues):
                cols = slice(c * LANES, (c + 1) * LANES)
                o_ref[rows, cols] = fn(acc[:, cols], rows).astype(o_ref.dtype)

    n_chunks = PROJ_TN // LANES
    q_gain, k_gain = qk_ref[0:1, :], qk_ref[1:2, :]

    @pl.when(j < 8)
    def _():
        run([gate] * n_chunks)

    @pl.when((j >= 8) & (j < 10))
    def _():
        run([axial(q_gain, QK_SCALE_LOG2E)] * n_chunks)

    @pl.when(j == 10)
    def _():
        run([axial(k_gain, 1.0), axial(k_gain, 1.0), plain, plain])

    @pl.when((j >= 11) & (j < 13))
    def _():
        run([partial_rot(QK_SCALE_LOG2E)] * n_chunks)

    @pl.when((j >= 13) & (j < 15))
    def _():
        run([partial_rot(1.0)] * n_chunks)

    @pl.when(j >= 15)
    def _():
        run([plain] * n_chunks)


def _rope_tables(seq):
    t = jnp.arange(seq, dtype=jnp.int32)
    row = (t // GRID_W).astype(F32)
    col = (t % GRID_W).astype(F32)
    lane = jnp.arange(LANES, dtype=jnp.int32)
    half = HEAD_DIM // 4
    inv_a = jnp.power(AXIAL_THETA, -2.0 * (lane % half).astype(F32) / (HEAD_DIM // 2))
    pos_a = jnp.where(lane[None, :] < HEAD_DIM // 2, row[:, None], col[:, None])
    ang_a = pos_a * inv_a[None, :]
    sign_a = jnp.where((lane % (2 * half)) < half, -1.0, 1.0).astype(F32)
    ca, sa = jnp.cos(ang_a), jnp.sin(ang_a) * sign_a[None, :]
    hb = ROT_DIM_B // 2
    inv_b = jnp.power(ROPE_THETA, -2.0 * (lane % hb).astype(F32) / ROT_DIM_B)
    ang_b = t.astype(F32)[:, None] * inv_b[None, :]
    rot = (lane < ROT_DIM_B)[None, :]
    sign_b = jnp.where((lane % (2 * hb)) < hb, -1.0, 1.0).astype(F32)
    cb = jnp.where(rot, jnp.cos(ang_b), 1.0)
    sb = jnp.where(rot, jnp.sin(ang_b) * sign_b[None, :], 0.0)
    return ca, sa, cb, sb


def _proj(x, g, w, tables, qk_gain, seq, tm=1024):
    T = x.shape[0]
    tm = min(tm, seq)
    nseq = seq // tm
    tab_spec = pl.BlockSpec((tm, LANES), lambda i, j: (i % nseq, 0))
    return pl.pallas_call(
        _proj_kernel,
        grid=(T // tm, IN_COLS // PROJ_TN),
        in_specs=[
            pl.BlockSpec((tm, D_MODEL), lambda i, j: (i, 0)),
            pl.BlockSpec((1, D_MODEL), lambda i, j: (0, 0)),
            pl.BlockSpec((D_MODEL, PROJ_TN), lambda i, j: (0, j)),
            tab_spec, tab_spec, tab_spec, tab_spec,
            pl.BlockSpec((2, LANES), lambda i, j: (0, 0)),
        ],
        out_specs=pl.BlockSpec((tm, PROJ_TN), lambda i, j: (i, j)),
        out_shape=jax.ShapeDtypeStruct((T, IN_COLS), BF16),
        scratch_shapes=[pltpu.VMEM((tm, D_MODEL), BF16)],
        compiler_params=_cparams(("parallel", "arbitrary")),
        name="proj",
    )(x, g, w, *tables, qk_gain)


def _nt_dot(a, b):
    return lax.dot_general(a, b, (((1,), (1,)), ((), ())), preferred_element_type=F32)


def _lane_tile(x, n):
    return jnp.concatenate([x] * n, axis=1)


def _softmax_block(s, m_ref):
    m_prev = m_ref[...]
    m_new = jnp.maximum(m_prev, jnp.max(s, axis=-1, keepdims=True))
    alpha = jnp.exp2(m_prev - m_new)
    p = jnp.exp2(s - _lane_tile(m_new, s.shape[1] // LANES))
    m_ref[...] = m_new
    return alpha, p


def _attn_a_kernel(q_ref, k_ref, v_ref, o_ref, m_ref, acc_ref, vaug_ref, *, tk, unroll):
    tq = q_ref.shape[0]
    group = N_HEADS_A // N_KV_A

    @pl.when(pl.program_id(2) == 0)
    def _():
        vaug_ref[:, :HEAD_DIM] = v_ref[...]
        vaug_ref[:, HEAD_DIM:] = jnp.ones(v_ref.shape, BF16)

    m_ref[...] = jnp.full(m_ref.shape, NEG_BIG, F32)
    acc_ref[...] = jnp.zeros(acc_ref.shape, F32)

    def body(i, carry):
        for u in range(unroll):
            off = pl.multiple_of((i * unroll + u) * tk, tk)
            k = k_ref[pl.ds(off, tk), :]
            v = vaug_ref[pl.ds(off, tk), :]
            for g in range(group):
                rows = pl.ds(g * tq, tq)
                s = _nt_dot(q_ref[:, g * HEAD_DIM:(g + 1) * HEAD_DIM], k)
                alpha, p = _softmax_block(s, m_ref.at[rows])
                pv = jnp.dot(p.astype(BF16), v, preferred_element_type=F32)
                acc_ref[rows, :] = _lane_tile(alpha, 2) * acc_ref[rows, :] + pv
        return carry

    lax.fori_loop(0, k_ref.shape[0] // (tk * unroll), body, 0)
    o = acc_ref[:, :HEAD_DIM] / acc_ref[:, HEAD_DIM:]
    for g in range(group):
        o_ref[:, g * HEAD_DIM:(g + 1) * HEAD_DIM] = o[g * tq:(g + 1) * tq].astype(o_ref.dtype)


def _single_buffered(shape, index_map):
    return pl.BlockSpec(shape, index_map, pipeline_mode=pl.Buffered(1))


def _attn_a(p, batch, seq, tq=512, tk=2048):
    T = p.shape[0]
    group = N_HEADS_A // N_KV_A
    qw = group * HEAD_DIM
    tq = min(tq, seq)
    tk = min(tk, seq)
    nq = seq // tq
    unroll = 2 if (seq // tk) % 2 == 0 else 1
    return pl.pallas_call(
        functools.partial(_attn_a_kernel, tk=tk, unroll=unroll),
        grid=(batch, N_KV_A, nq),
        in_specs=[
            pl.BlockSpec((tq, qw), lambda b, h, i: (b * nq + i, COL_QA // qw + h)),
            _single_buffered((seq, HEAD_DIM), lambda b, h, i: (b, COL_KA // HEAD_DIM + h)),
            _single_buffered((seq, HEAD_DIM), lambda b, h, i: (b, COL_VA // HEAD_DIM + h)),
        ],
        out_specs=pl.BlockSpec((tq, qw), lambda b, h, i: (b * nq + i, h)),
        out_shape=jax.ShapeDtypeStruct((T, N_HEADS_A * HEAD_DIM), BF16),
        scratch_shapes=[pltpu.VMEM((group * tq, LANES), F32),
                        pltpu.VMEM((group * tq, 2 * HEAD_DIM), F32),
                        pltpu.VMEM((seq, 2 * HEAD_DIM), BF16)],
        compiler_params=_cparams(("parallel", "parallel", "arbitrary")),
        name="attn_a",
    )(p, p, p)


def _attn_b_kernel(lam_ref, q_ref, k_ref, v_ref, sub_ref, o_ref, m_ref, l_ref, acc_ref, *, tk, sub, unroll, lam_init):
    tq = q_ref.shape[0]
    m_ref[...] = jnp.full(m_ref.shape, NEG_BIG, F32)
    l_ref[...] = jnp.zeros(l_ref.shape, F32)
    acc_ref[...] = jnp.zeros(acc_ref.shape, F32)

    def body(i, carry):
        for u in range(unroll):
            off = pl.multiple_of((i * unroll + u) * tk, tk)
            v = v_ref[pl.ds(off, tk), :]
            for c in range(2):
                cols = slice(c * HEAD_DIM, (c + 1) * HEAD_DIM)
                k = k_ref[pl.ds(off, tk), cols]
                for r in range(tq // sub):
                    rows = pl.ds(c * tq + r * sub, sub)
                    s = _nt_dot(q_ref[r * sub:(r + 1) * sub, cols], k)
                    alpha, p = _softmax_block(s, m_ref.at[rows])
                    l_ref[rows, :] = alpha * l_ref[rows, :] + jnp.sum(p, axis=-1, keepdims=True)
                    pv = jnp.dot(p.astype(BF16), v, preferred_element_type=F32)
                    acc_ref[rows, :] = _lane_tile(alpha, 2) * acc_ref[rows, :] + pv
        return carry

    lax.fori_loop(0, k_ref.shape[0] // (tk * unroll), body, 0)
    lv = lam_ref[...]
    lam = (jnp.exp(jnp.sum(lv[0:1] * lv[1:2], axis=-1, keepdims=True))
           - jnp.exp(jnp.sum(lv[2:3] * lv[3:4], axis=-1, keepdims=True)) + lam_init)
    o = acc_ref[...] / _lane_tile(l_ref[...], 2)
    od = o[:tq] - lam * o[tq:]
    o_ref[...] = (_rms(od, sub_ref[...]) * (1.0 - lam_init)).astype(o_ref.dtype)


def _attn_b(p, lam_vecs, subln, batch, seq, lam_init, tq=1024, sub=512, tk=2048):
    T = p.shape[0]
    vw = 2 * HEAD_DIM
    tq = min(tq, seq)
    sub = min(sub, tq)
    tk = min(tk, seq)
    nq = seq // tq
    unroll = 2 if (seq // tk) % 2 == 0 else 1
    return pl.pallas_call(
        functools.partial(_attn_b_kernel, tk=tk, sub=sub, unroll=unroll, lam_init=lam_init),
        grid=(batch, N_HEADS_B, nq),
        in_specs=[
            pl.BlockSpec((4, HEAD_DIM), lambda b, h, i: (0, 0)),
            pl.BlockSpec((tq, vw), lambda b, h, i: (b * nq + i, COL_QB // vw + h)),
            _single_buffered((seq, vw), lambda b, h, i: (b, COL_KB // vw + h)),
            _single_buffered((seq, vw), lambda b, h, i: (b, COL_VB // vw + h)),
            pl.BlockSpec((1, vw), lambda b, h, i: (0, 0)),
        ],
        out_specs=pl.BlockSpec((tq, vw), lambda b, h, i: (b * nq + i, h)),
        out_shape=jax.ShapeDtypeStruct((T, N_HEADS_B * vw), BF16),
        scratch_shapes=[pltpu.VMEM((2 * tq, LANES), F32), pltpu.VMEM((2 * tq, LANES), F32),
                        pltpu.VMEM((2 * tq, vw), F32)],
        compiler_params=_cparams(("parallel", "parallel", "parallel")),
        name="attn_b",
    )(lam_vecs, p, p, p, subln)


def _merge_kernel(x_ref, oa_ref, ob_ref, ga_ref, gb_ref, wa_ref, wb_ref, wo_ref, o_ref):
    a = jnp.dot(oa_ref[...], wa_ref[...], preferred_element_type=F32)
    b = jnp.dot(ob_ref[...], wb_ref[...], preferred_element_type=F32)
    merged = ga_ref[...].astype(F32) * a + gb_ref[...].astype(F32) * b
    o_ref[...] = x_ref[...] + jnp.dot(merged.astype(BF16), wo_ref[...], preferred_element_type=F32)


def _resident(shape):
    return pl.BlockSpec(shape, lambda *_: (0,) * len(shape), pipeline_mode=pl.Buffered(1))


def _merge(x, oa, ob, p, wa, wb, wo, tm=256):
    T = x.shape[0]
    return pl.pallas_call(
        _merge_kernel,
        grid=(T // tm,),
        in_specs=[
            pl.BlockSpec((tm, D_MODEL), lambda i: (i, 0)),
            pl.BlockSpec((tm, oa.shape[1]), lambda i: (i, 0)),
            pl.BlockSpec((tm, ob.shape[1]), lambda i: (i, 0)),
            pl.BlockSpec((tm, D_MODEL), lambda i: (i, COL_GA // D_MODEL)),
            pl.BlockSpec((tm, D_MODEL), lambda i: (i, COL_GB // D_MODEL)),
            _resident(wa.shape), _resident(wb.shape), _resident(wo.shape),
        ],
        out_specs=pl.BlockSpec((tm, D_MODEL), lambda i: (i, 0)),
        out_shape=jax.ShapeDtypeStruct((T, D_MODEL), F32),
        compiler_params=_cparams(("parallel",)),
        name="merge",
    )(x, oa, ob, p, p, wa, wb, wo)


def _norm_matmul_kernel(x_ref, g_ref, w_ref, o_ref):
    xn = _rms(x_ref[...], g_ref[...]).astype(BF16)
    o_ref[...] = jnp.dot(xn, w_ref[...], preferred_element_type=F32).astype(o_ref.dtype)


def _norm_matmul(x, g, w, tm=256):
    T, K = x.shape
    N = w.shape[1]
    return pl.pallas_call(
        _norm_matmul_kernel,
        grid=(T // tm,),
        in_specs=[pl.BlockSpec((tm, K), lambda i: (i, 0)), pl.BlockSpec((1, K), lambda i: (0, 0)),
                  pl.BlockSpec((K, N), lambda i: (0, 0))],
        out_specs=pl.BlockSpec((tm, N), lambda i: (i, 0)),
        out_shape=jax.ShapeDtypeStruct((T, N), BF16),
        compiler_params=_cparams(("parallel",)),
        name="mem_kv",
    )(x, g, w)


def _cross_kernel(x_ref, g_ref, wq_ref, kv_ref, wo_ref, o_ref):
    x = x_ref[...]
    hn = _rms(x, g_ref[...]).astype(BF16)
    q = (jnp.dot(hn, wq_ref[...], preferred_element_type=F32) * QK_SCALE).astype(BF16)
    heads = []
    for h in range(N_HEADS_X):
        lo = h * HEAD_DIM
        s = _nt_dot(q[:, lo:lo + HEAD_DIM], kv_ref[:, lo:lo + HEAD_DIM])
        e = jnp.exp(s - jnp.max(s, axis=-1, keepdims=True))
        pr = (e / jnp.sum(e, axis=-1, keepdims=True)).astype(BF16)
        heads.append(jnp.dot(pr, kv_ref[:, WIDTH_X + lo:WIDTH_X + lo + HEAD_DIM], preferred_element_type=F32))
    o = jnp.concatenate(heads, axis=1).astype(BF16)
    o_ref[...] = x + jnp.dot(o, wo_ref[...], preferred_element_type=F32)


def _cross(x, g, wq, kv, wo, seq, tm=512):
    T = x.shape[0]
    tm = min(tm, seq)
    per_seq = seq // tm
    return pl.pallas_call(
        _cross_kernel,
        grid=(T // tm,),
        in_specs=[
            pl.BlockSpec((tm, D_MODEL), lambda i: (i, 0)),
            pl.BlockSpec((1, D_MODEL), lambda i: (0, 0)),
            pl.BlockSpec(wq.shape, lambda i: (0, 0)),
            pl.BlockSpec((MEM_LEN, 2 * WIDTH_X), lambda i: (i // per_seq, 0)),
            pl.BlockSpec(wo.shape, lambda i: (0, 0)),
        ],
        out_specs=pl.BlockSpec((tm, D_MODEL), lambda i: (i, 0)),
        out_shape=jax.ShapeDtypeStruct((T, D_MODEL), F32),
        compiler_params=_cparams(("parallel",)),
        name="cross",
    )(x, g, wq, kv, wo)


def _router_kernel(x_ref, g_ref, wr_ref, br_ref, cin_ref, hn_ref, ids_ref, wts_ref, rank_ref, cnt_ref, run_ref):
    i = pl.program_id(0)

    @pl.when(i == 0)
    def _():
        run_ref[...] = cin_ref[...]

    hn = _rms(x_ref[...], g_ref[...])
    hn_ref[...] = hn
    logits = jnp.dot(hn, wr_ref[...], preferred_element_type=F32, precision=lax.Precision.HIGHEST) + br_ref[...]
    tm = logits.shape[0]
    lane = lax.broadcasted_iota(jnp.int32, logits.shape, 1)
    lane_f = lane.astype(F32)
    vals = logits
    top_v, top_i, sels = [], [], []
    for _ in range(TOP_K):
        mk = jnp.max(vals, axis=-1, keepdims=True)
        idx = jnp.min(jnp.where(vals == mk, lane_f, float(LANES)), axis=-1, keepdims=True)
        sel = lane_f == idx
        top_v.append(mk)
        top_i.append(idx.astype(jnp.int32))
        sels.append(sel)
        vals = jnp.where(sel, 2.0 * NEG_BIG, vals)
    es = [jnp.exp(v - top_v[0]) for v in top_v]
    denom = es[0] + es[1] + es[2] + es[3]
    onehot = (sels[0] | sels[1] | sels[2] | sels[3]).astype(F32)
    r = lax.broadcasted_iota(jnp.int32, (tm, tm), 0)
    c = lax.broadcasted_iota(jnp.int32, (tm, tm), 1)
    tri = (c < r).astype(BF16)
    prefix = jnp.dot(tri, onehot.astype(BF16), preferred_element_type=F32)
    rank_te = prefix + run_ref[0:1, :]
    ids = jnp.zeros(logits.shape, jnp.int32)
    wts = jnp.zeros(logits.shape, F32)
    rank = jnp.zeros(logits.shape, jnp.int32)
    for k in range(TOP_K):
        rk = jnp.sum(jnp.where(sels[k], rank_te, 0.0), axis=-1, keepdims=True).astype(jnp.int32)
        ids = jnp.where(lane == k, top_i[k], ids)
        wts = jnp.where(lane == k, es[k] / denom, wts)
        rank = jnp.where(lane == k, rk, rank)
    ids_ref[...] = ids
    wts_ref[...] = wts
    rank_ref[...] = rank
    run_ref[...] = run_ref[...] + jnp.sum(onehot, axis=0, keepdims=True)
    cnt_ref[...] = run_ref[...]


def _router(x, g, wr, br, counts_in, tm=512):
    T = x.shape[0]
    row = pl.BlockSpec((tm, LANES), lambda i: (i, 0))
    return pl.pallas_call(
        _router_kernel,
        grid=(T // tm,),
        in_specs=[pl.BlockSpec((tm, D_MODEL), lambda i: (i, 0)), pl.BlockSpec((1, D_MODEL), lambda i: (0, 0)),
                  pl.BlockSpec((D_MODEL, LANES), lambda i: (0, 0)), pl.BlockSpec((1, LANES), lambda i: (0, 0)),
                  pl.BlockSpec((8, LANES), lambda i: (0, 0))],
        out_specs=[pl.BlockSpec((tm, D_MODEL), lambda i: (i, 0)), row, row, row,
                   pl.BlockSpec((8, LANES), lambda i: (0, 0))],
        out_shape=[jax.ShapeDtypeStruct((T, D_MODEL), F32), jax.ShapeDtypeStruct((T, LANES), jnp.int32),
                   jax.ShapeDtypeStruct((T, LANES), F32), jax.ShapeDtypeStruct((T, LANES), jnp.int32),
                   jax.ShapeDtypeStruct((8, LANES), F32)],
        scratch_shapes=[pltpu.VMEM((8, LANES), F32)],
        compiler_params=_cparams(("arbitrary",)),
        name="router",
    )(x, g, wr, br, counts_in)


DISPATCH_CHUNK = 512


ISSUE_UNROLL = 2


def _dispatch_kernel(slot_ref, hn_ref, xs_in_ref, xs_ref, sem):
    del xs_in_ref

    def issue(j, carry):
        for u in range(ISSUE_UNROLL):
            c = j * ISSUE_UNROLL + u
            for k in range(TOP_K):
                slot = slot_ref[0, 0, c * TOP_K + k]
                pltpu.make_async_copy(hn_ref.at[pl.ds(c, 1)], xs_ref.at[pl.ds(slot, 1)], sem).start()
        return carry

    lax.fori_loop(0, DISPATCH_CHUNK // ISSUE_UNROLL, issue, 0)
    for k in range(TOP_K):
        pltpu.make_async_copy(hn_ref, xs_ref.at[pl.ds(0, DISPATCH_CHUNK)], sem).wait()


def _dispatch(hn, slot4, xs0):
    T = hn.shape[0]
    n = T // DISPATCH_CHUNK
    n_slots = xs0.shape[0]
    idx_spec = pl.BlockSpec((1, 1, DISPATCH_CHUNK * TOP_K), lambda i: (i, 0, 0), memory_space=pltpu.SMEM)
    return pl.pallas_call(
        _dispatch_kernel,
        grid=(n,),
        in_specs=[idx_spec, pl.BlockSpec((DISPATCH_CHUNK, D_MODEL), lambda i: (i, 0)),
                  pl.BlockSpec(memory_space=pl.ANY)],
        out_specs=pl.BlockSpec(memory_space=pl.ANY),
        scratch_shapes=[pltpu.SemaphoreType.DMA(())],
        out_shape=jax.ShapeDtypeStruct((n_slots, D_MODEL), F32),
        input_output_aliases={2: 0},
        compiler_params=_cparams(("arbitrary",)),
        name="dispatch",
    )(slot4.reshape(n, 1, -1), hn, xs0)


SPLIT_TN = 512


def _split_glu_kernel(w_ref, perm_ref, g_ref, l_ref):
    r = jnp.dot(w_ref[0].astype(BF16), perm_ref[...], preferred_element_type=F32)
    half = SPLIT_TN // 2
    g_ref[0] = r[:, :half].astype(BF16)
    l_ref[0] = r[:, half:].astype(BF16)


def _split_glu(w1):
    E, K, N2 = w1.shape
    half = SPLIT_TN // 2
    src = jnp.arange(SPLIT_TN, dtype=jnp.int32)
    dst = jnp.where(src % 2 == 0, src // 2, half + src // 2)
    perm = (dst[:, None] == jnp.arange(SPLIT_TN, dtype=jnp.int32)[None, :]).astype(BF16)
    out = jax.ShapeDtypeStruct((E, K, N2 // 2), BF16)
    return pl.pallas_call(
        _split_glu_kernel,
        grid=(E, N2 // SPLIT_TN),
        in_specs=[pl.BlockSpec((1, K, SPLIT_TN), lambda e, j: (e, 0, j)),
                  pl.BlockSpec((SPLIT_TN, SPLIT_TN), lambda e, j: (0, 0))],
        out_specs=[pl.BlockSpec((1, K, half), lambda e, j: (e, 0, j)),
                   pl.BlockSpec((1, K, half), lambda e, j: (e, 0, j))],
        out_shape=[out, out],
        compiler_params=_cparams(("parallel", "parallel")),
        name="split_glu",
    )(w1, perm)


MOE_TM = 512
MOE_TF = 1024
MOE_SUB = 512


MOE_GROUP = 1


def _experts_kernel(ge_ref, tv_ref, xs_ref, w1g_ref, w1l_ref, b1g_ref, b1l_ref, w2_ref, b2_ref, ys_ref, xb_ref, acc_ref):
    i = pl.program_id(0)
    f = pl.program_id(1)
    s = pl.program_id(2)
    last = pl.num_programs(1) - 1
    valid = tv_ref[i * MOE_GROUP + s] > 0

    @pl.when(valid)
    def _():
        @pl.when(f == 0)
        def _():
            xb_ref[s] = xs_ref[...].astype(BF16)

        xb = xb_ref[s]
        part = None
        for c in range(MOE_TF // MOE_SUB):
            cols = slice(c * MOE_SUB, (c + 1) * MOE_SUB)
            glu = jnp.dot(xb, w1g_ref[0, :, cols], preferred_element_type=F32) + b1g_ref[0, :, cols]
            lin = jnp.dot(xb, w1l_ref[0, :, cols], preferred_element_type=F32) + b1l_ref[0, :, cols]
            glu = jnp.minimum(glu, SWIGLU_LIMIT)
            lin = jnp.clip(lin, -SWIGLU_LIMIT, SWIGLU_LIMIT)
            act = glu * jax.nn.sigmoid(SWIGLU_ALPHA * glu) * (lin + 1.0)
            sub_part = jnp.dot(act.astype(BF16), w2_ref[0, cols, :], preferred_element_type=F32)
            part = sub_part if part is None else part + sub_part

        @pl.when(f == 0)
        def _():
            acc_ref[s] = part

        @pl.when((f > 0) & (f < last))
        def _():
            acc_ref[s] += part

        @pl.when(f == last)
        def _():
            ys_ref[...] = acc_ref[s] + part + b2_ref[0]

    @pl.when(jnp.logical_not(valid) & (f == last))
    def _():
        ys_ref[...] = jnp.zeros(ys_ref.shape, F32)


def _experts(xs, group_expert, tile_valid, w1g, w1l, b1g, b1l, w2, b2):
    n_slots = xs.shape[0]
    n_groups = n_slots // (MOE_TM * MOE_GROUP)
    nf = D_FF // MOE_TF
    assert nf >= 2
    G = MOE_GROUP

    def xs_map(i, f, s, ge, tv):
        return (jnp.where(f == 0, i * G + s, i * G + G - 1), 0)

    def ys_map(i, f, s, ge, tv):
        return (jnp.where(f == nf - 1, i * G + s, i * G), 0)

    return pl.pallas_call(
        _experts_kernel,
        grid_spec=pltpu.PrefetchScalarGridSpec(
            num_scalar_prefetch=2,
            grid=(n_groups, nf, G),
            in_specs=[
                pl.BlockSpec((MOE_TM, D_MODEL), xs_map),
                pl.BlockSpec((1, D_MODEL, MOE_TF), lambda i, f, s, ge, tv: (ge[i], 0, f)),
                pl.BlockSpec((1, D_MODEL, MOE_TF), lambda i, f, s, ge, tv: (ge[i], 0, f)),
                pl.BlockSpec((1, 1, MOE_TF), lambda i, f, s, ge, tv: (ge[i], 0, f)),
                pl.BlockSpec((1, 1, MOE_TF), lambda i, f, s, ge, tv: (ge[i], 0, f)),
                pl.BlockSpec((1, MOE_TF, D_MODEL), lambda i, f, s, ge, tv: (ge[i], f, 0)),
                pl.BlockSpec((1, 1, D_MODEL), lambda i, f, s, ge, tv: (ge[i], 0, 0)),
            ],
            out_specs=pl.BlockSpec((MOE_TM, D_MODEL), ys_map),
            scratch_shapes=[pltpu.VMEM((G, MOE_TM, D_MODEL), BF16), pltpu.VMEM((G, MOE_TM, D_MODEL), F32)],
        ),
        out_shape=jax.ShapeDtypeStruct((n_slots, D_MODEL), F32),
        compiler_params=_cparams(("parallel", "arbitrary", "arbitrary")),
        name="experts",
    )(group_expert, tile_valid, xs, w1g, w1l, b1g, b1l, w2, b2)


COMBINE_CHUNK = 128


def _combine_kernel(slot_ref, slot_nxt_ref, x_ref, wts_ref, g_ref, ys_ref, o_ref, buf_ref, sem):
    i = pl.program_id(0)
    n = pl.num_programs(0)
    cur = i % 2

    def gather(slots, buf):
        def issue(j, carry):
            for u in range(ISSUE_UNROLL):
                c = j * ISSUE_UNROLL + u
                for k in range(TOP_K):
                    slot = slots[0, 0, c * TOP_K + k]
                    pltpu.make_async_copy(ys_ref.at[pl.ds(slot, 1)], buf_ref.at[buf, k, pl.ds(c, 1)],
                                          sem.at[buf]).start()
            return carry
        lax.fori_loop(0, COMBINE_CHUNK // ISSUE_UNROLL, issue, 0)

    @pl.when(i == 0)
    def _():
        gather(slot_ref, 0)

    @pl.when(i + 1 < n)
    def _():
        gather(slot_nxt_ref, 1 - cur)

    for k in range(TOP_K):
        pltpu.make_async_copy(ys_ref.at[pl.ds(0, COMBINE_CHUNK)], buf_ref.at[cur, k], sem.at[cur]).wait()
    wts = wts_ref[...]
    y = x_ref[...]
    for k in range(TOP_K):
        y = y + wts[:, k:k + 1] * buf_ref[cur, k]
    o_ref[...] = _rms(y, g_ref[...])


def _combine(x, wts, g, ys, slot4):
    T = x.shape[0]
    n = T // COMBINE_CHUNK
    idx_shape = (1, 1, COMBINE_CHUNK * TOP_K)
    idx_spec = pl.BlockSpec(idx_shape, lambda i: (i, 0, 0), memory_space=pltpu.SMEM)
    nxt_spec = pl.BlockSpec(idx_shape, lambda i: (jnp.minimum(i + 1, n - 1), 0, 0), memory_space=pltpu.SMEM)
    slots = slot4.reshape(n, 1, -1)
    return pl.pallas_call(
        _combine_kernel,
        grid=(n,),
        in_specs=[idx_spec, nxt_spec,
                  pl.BlockSpec((COMBINE_CHUNK, D_MODEL), lambda i: (i, 0)),
                  pl.BlockSpec((COMBINE_CHUNK, LANES), lambda i: (i, 0)),
                  pl.BlockSpec((1, D_MODEL), lambda i: (0, 0)),
                  pl.BlockSpec(memory_space=pl.ANY)],
        out_specs=pl.BlockSpec((COMBINE_CHUNK, D_MODEL), lambda i: (i, 0)),
        scratch_shapes=[pltpu.VMEM((2, TOP_K, COMBINE_CHUNK, D_MODEL), F32), pltpu.SemaphoreType.DMA((2,))],
        out_shape=jax.ShapeDtypeStruct((T, D_MODEL), F32),
        compiler_params=_cparams(("arbitrary",)),
        name="combine",
    )(slots, slots, x, wts, g, ys)


def _mix_and_cross(x3, mem3, w, lam_init):
    batch, seq, _ = x3.shape
    x = x3.reshape(batch * seq, D_MODEL)
    mem = mem3.reshape(batch * MEM_LEN, D_MODEL)
    p = _proj(x, w["norm_mix"], w["w_in"], _rope_tables(seq), w["qk_gain"], seq)
    oa = _attn_a(p, batch, seq)
    ob = _attn_b(p, w["lam_vecs"], w["subln_b"], batch, seq, lam_init)
    x = _merge(x, oa, ob, p, w["w_branch_a"], w["w_branch_b"], w["w_out"])
    kv = _norm_matmul(mem, w["norm_mem"], w["w_ckv"])
    return _cross(x, w["norm_cross"], w["w_cq"], kv, w["w_co"], seq)


def _moe_and_final_norm(xs_list, w):
    counts = jnp.zeros((8, LANES), F32)
    routed = []
    for x in xs_list:
        hn, ids, gate_w, rank, counts = _router(x, w["norm_moe"], w["w_router"], w["b_router"], counts)
        routed.append((hn, ids[:, :TOP_K], rank[:, :TOP_K], gate_w))
    rows_per_group = MOE_TM * MOE_GROUP
    cnt = counts[0, :N_EXPERTS].astype(jnp.int32)
    padded = ((cnt + rows_per_group - 1) // rows_per_group) * rows_per_group
    group_end = jnp.cumsum(padded)
    group_start = group_end - padded
    total = sum(x.shape[0] for x in xs_list) * TOP_K
    n_groups = total // rows_per_group + N_EXPERTS
    lo = jnp.arange(n_groups, dtype=jnp.int32) * rows_per_group
    group_expert = jnp.sum((lo[:, None] >= group_end[None, :]).astype(jnp.int32), axis=1)
    last_expert = jnp.max(jnp.where(cnt > 0, jnp.arange(N_EXPERTS, dtype=jnp.int32), 0))
    group_expert = jnp.minimum(group_expert, last_expert).astype(jnp.int32)
    tile_lo = jnp.arange(n_groups * MOE_GROUP, dtype=jnp.int32) * MOE_TM
    tile_e = jnp.repeat(group_expert, MOE_GROUP)
    tile_valid = (tile_lo < group_start[tile_e] + cnt[tile_e]).astype(jnp.int32)
    slot4s = [group_start[ids4] + rank4 for _, ids4, rank4, _ in routed]
    xs = jnp.zeros((n_groups * rows_per_group, D_MODEL), F32)
    for (hn, _, _, _), slot4 in zip(routed, slot4s):
        xs = _dispatch(hn, slot4, xs)
    ys = _experts(xs, group_expert, tile_valid, w["w1g"], w["w1l"], w["b1g"], w["b1l"], w["w2"], w["b2"])
    return [_combine(x, gate_w, w["norm_final"], ys, slot4)
            for x, (_, _, _, gate_w), slot4 in zip(xs_list, routed, slot4s)]


def kernel(x_prompt, x_sample, mem_prompt, mem_sample, norm_mix, w_in, q_norm_a, k_norm_a, lambda_q1, lambda_k1, lambda_q2, lambda_k2, subln_b, w_branch_a, w_branch_b, w_out, norm_cross, norm_mem, w_cq, w_ckv, w_co, norm_moe, w_router, b_router, w_exp1, b_exp1, w_exp2, b_exp2, norm_final):
    assert DEPTH == 1
    wts = []
    for l in range(DEPTH):
        wr = jnp.zeros((D_MODEL, LANES), F32).at[:, :N_EXPERTS].set(w_router[l])
        br = jnp.full((1, LANES), NEG_BIG, F32).at[0, :N_EXPERTS].set(b_router[l])
        w1g, w1l = _split_glu(w_exp1[l])
        wts.append(dict(
            norm_mix=norm_mix[l][None, :],
            w_in=jnp.concatenate([w_in[l][:, REF_GATE_START:], w_in[l][:, :REF_GATE_START]], axis=1).astype(BF16),
            qk_gain=jnp.stack([q_norm_a[l], k_norm_a[l]]),
            lam_vecs=jnp.stack([lambda_q1[l], lambda_k1[l], lambda_q2[l], lambda_k2[l]]),
            subln_b=subln_b[l][None, :],
            w_branch_a=w_branch_a[l].astype(BF16),
            w_branch_b=w_branch_b[l].astype(BF16),
            w_out=w_out[l].astype(BF16),
            norm_cross=norm_cross[l][None, :],
            norm_mem=norm_mem[l][None, :],
            w_cq=w_cq[l].astype(BF16),
            w_ckv=w_ckv[l].astype(BF16),
            w_co=w_co[l].astype(BF16),
            norm_moe=norm_moe[l][None, :],
            w_router=wr,
            b_router=br,
            w1g=w1g,
            w1l=w1l,
            b1g=b_exp1[l][:, None, 0::2],
            b1l=b_exp1[l][:, None, 1::2],
            w2=w_exp2[l].astype(BF16),
            b2=b_exp2[l][:, None, :],
            norm_final=norm_final[None, :],
        ))
    w = wts[0]
    lam_init = 0.8 - 0.6 * math.exp(-0.3 * 0)
    streams = [_mix_and_cross(x_prompt, mem_prompt, w, lam_init), _mix_and_cross(x_sample, mem_sample, w, lam_init)]
    y_prompt, y_sample = _moe_and_final_norm(streams, w)
    return (y_prompt.reshape(x_prompt.shape), y_sample.reshape(x_sample.shape))
```

```python
import functools
import math

import jax
import jax.numpy as jnp
from jax import lax
from jax.experimental import pallas as pl
from jax.experimental.pallas import tpu as pltpu

F32 = jnp.float32
BF16 = jnp.bfloat16

D_MODEL = 2048
DEPTH = 1
GRID_W = 64
RMS_EPS = 1e-6
N_HEADS_A = 8
N_KV_A = 2
HEAD_DIM = 128
AXIAL_THETA = 10000.0
N_HEADS_B = 4
ROPE_THETA = 500000.0
ROT_DIM_B = HEAD_DIM // 4
MEM_LEN = 256
N_HEADS_X = 4
WIDTH_X = N_HEADS_X * HEAD_DIM
N_EXPERTS = 32
TOP_K = 4
D_FF = D_MODEL
SWIGLU_ALPHA = 1.702
SWIGLU_LIMIT = 7.0
LANES = 128
IN_COLS = 8704
QK_SCALE = HEAD_DIM ** -0.5
QK_SCALE_LOG2E = QK_SCALE * math.log2(math.e)
NEG_BIG = -1e30

VMEM_LIMIT = 56 * 1024 * 1024

COL_GA = 0
COL_GB = 2048
COL_QA = 4096
COL_KA = 5120
COL_VA = 5376
COL_QB = 5632
COL_KB = 6656
COL_VB = 7680
REF_GATE_START = 4608


def _cparams(sem):
    return pltpu.CompilerParams(dimension_semantics=sem, vmem_limit_bytes=VMEM_LIMIT)


def _rms(x, g):
    return x * lax.rsqrt(jnp.mean(x * x, axis=-1, keepdims=True) + RMS_EPS) * g


def _rotate(x, cos, sin_signed, shift):
    n = x.shape[-1]
    lane = lax.broadcasted_iota(jnp.int32, x.shape, 1)
    up = pltpu.roll(x, n - shift, 1)
    dn = pltpu.roll(x, shift, 1)
    partner = jnp.where((lane % (2 * shift)) < shift, up, dn)
    return x * cos + partner * sin_signed


PROJ_TN = 512


def _proj_kernel(x_ref, g_ref, w_ref, ca_ref, sa_ref, cb_ref, sb_ref, qk_ref, o_ref, xn_ref):
    j = pl.program_id(1)

    @pl.when(j == 0)
    def _():
        xn_ref[...] = _rms(x_ref[...], g_ref[...]).astype(BF16)

    tm = xn_ref.shape[0]
    n_sub = 2 if tm % 16 == 0 else 1
    sub = tm // n_sub

    def axial(gain, scale):
        def f(a, rows):
            r = _rotate(_rms(a, gain), ca_ref[rows, :], sa_ref[rows, :], HEAD_DIM // 4)
            return r * scale if scale != 1.0 else r
        return f

    def partial_rot(scale):
        def f(a, rows):
            r = _rotate(a, cb_ref[rows, :], sb_ref[rows, :], ROT_DIM_B // 2)
            return r * scale if scale != 1.0 else r
        return f

    def plain(a, rows):
        return a

    def gate(a, rows):
        return jax.nn.sigmoid(a)

    def run(epilogues):
        for r in range(n_sub):
            rows = slice(r * sub, (r + 1) * sub)
            acc = jnp.dot(xn_ref[rows, :], w_ref[...], preferred_element_type=F32)
            for c, fn in enumerate(epilogues):
                cols = slice(c * LANES, (c + 1) * LANES)
                o_ref[rows, cols] = fn(acc[:, cols], rows).astype(o_ref.dtype)

    n_chunks = PROJ_TN // LANES
    q_gain, k_gain = qk_ref[0:1, :], qk_ref[1:2, :]

    @pl.when(j < 8)
    def _():
        run([gate] * n_chunks)

    @pl.when((j >= 8) & (j < 10))
    def _():
        run([axial(q_gain, QK_SCALE_LOG2E)] * n_chunks)

    @pl.when(j == 10)
    def _():
        run([axial(k_gain, 1.0), axial(k_gain, 1.0), plain, plain])

    @pl.when((j >= 11) & (j < 13))
    def _():
        run([partial_rot(QK_SCALE_LOG2E)] * n_chunks)

    @pl.when((j >= 13) & (j < 15))
    def _():
        run([partial_rot(1.0)] * n_chunks)

    @pl.when(j >= 15)
    def _():
        run([plain] * n_chunks)


def _rope_tables(seq):
    t = jnp.arange(seq, dtype=jnp.int32)
    row = (t // GRID_W).astype(F32)
    col = (t % GRID_W).astype(F32)
    lane = jnp.arange(LANES, dtype=jnp.int32)
    half = HEAD_DIM // 4
    inv_a = jnp.power(AXIAL_THETA, -2.0 * (lane % half).astype(F32) / (HEAD_DIM // 2))
    pos_a = jnp.where(lane[None, :] < HEAD_DIM // 2, row[:, None], col[:, None])
    ang_a = pos_a * inv_a[None, :]
    sign_a = jnp.where((lane % (2 * half)) < half, -1.0, 1.0).astype(F32)
    ca, sa = jnp.cos(ang_a), jnp.sin(ang_a) * sign_a[None, :]
    hb = ROT_DIM_B // 2
    inv_b = jnp.power(ROPE_THETA, -2.0 * (lane % hb).astype(F32) / ROT_DIM_B)
    ang_b = t.astype(F32)[:, None] * inv_b[None, :]
    rot = (lane < ROT_DIM_B)[None, :]
    sign_b = jnp.where((lane % (2 * hb)) < hb, -1.0, 1.0).astype(F32)
    cb = jnp.where(rot, jnp.cos(ang_b), 1.0)
    sb = jnp.where(rot, jnp.sin(ang_b) * sign_b[None, :], 0.0)
    return ca, sa, cb, sb


def _proj(x, g, w, tables, qk_gain, seq, tm=1024):
    T = x.shape[0]
    tm = min(tm, seq)
    nseq = seq // tm
    tab_spec = pl.BlockSpec((tm, LANES), lambda i, j: (i % nseq, 0))
    return pl.pallas_call(
        _proj_kernel,
        grid=(T // tm, IN_COLS // PROJ_TN),
        in_specs=[
            pl.BlockSpec((tm, D_MODEL), lambda i, j: (i, 0)),
            pl.BlockSpec((1, D_MODEL), lambda i, j: (0, 0)),
            pl.BlockSpec((D_MODEL, PROJ_TN), lambda i, j: (0, j)),
            tab_spec, tab_spec, tab_spec, tab_spec,
            pl.BlockSpec((2, LANES), lambda i, j: (0, 0)),
        ],
        out_specs=pl.BlockSpec((tm, PROJ_TN), lambda i, j: (i, j)),
        out_shape=jax.ShapeDtypeStruct((T, IN_COLS), BF16),
        scratch_shapes=[pltpu.VMEM((tm, D_MODEL), BF16)],
        compiler_params=_cparams(("parallel", "arbitrary")),
        name="proj",
    )(x, g, w, *tables, qk_gain)


def _nt_dot(a, b):
    return lax.dot_general(a, b, (((1,), (1,)), ((), ())), preferred_element_type=F32)


def _lane_tile(x, n):
    return jnp.concatenate([x] * n, axis=1)


def _softmax_block(s, m_ref):
    m_prev = m_ref[...]
    m_new = jnp.maximum(m_prev, jnp.max(s, axis=-1, keepdims=True))
    alpha = jnp.exp2(m_prev - m_new)
    p = jnp.exp2(s - _lane_tile(m_new, s.shape[1] // LANES))
    m_ref[...] = m_new
    return alpha, p


def _attn_a_kernel(q_ref, k_ref, v_ref, o_ref, m_ref, acc_ref, vaug_ref, *, tk, unroll):
    tq = q_ref.shape[0]
    group = N_HEADS_A // N_KV_A

    @pl.when(pl.program_id(2) == 0)
    def _():
        vaug_ref[:, :HEAD_DIM] = v_ref[...]
        vaug_ref[:, HEAD_DIM:] = jnp.ones(v_ref.shape, BF16)

    m_ref[...] = jnp.full(m_ref.shape, NEG_BIG, F32)
    acc_ref[...] = jnp.zeros(acc_ref.shape, F32)

    def body(i, carry):
        for u in range(unroll):
            off = pl.multiple_of((i * unroll + u) * tk, tk)
            k = k_ref[pl.ds(off, tk), :]
            v = vaug_ref[pl.ds(off, tk), :]
            for g in range(group):
                rows = pl.ds(g * tq, tq)
                s = _nt_dot(q_ref[:, g * HEAD_DIM:(g + 1) * HEAD_DIM], k)
                alpha, p = _softmax_block(s, m_ref.at[rows])
                pv = jnp.dot(p.astype(BF16), v, preferred_element_type=F32)
                acc_ref[rows, :] = _lane_tile(alpha, 2) * acc_ref[rows, :] + pv
        return carry

    lax.fori_loop(0, k_ref.shape[0] // (tk * unroll), body, 0)
    o = acc_ref[:, :HEAD_DIM] / acc_ref[:, HEAD_DIM:]
    for g in range(group):
        o_ref[:, g * HEAD_DIM:(g + 1) * HEAD_DIM] = o[g * tq:(g + 1) * tq].astype(o_ref.dtype)


def _single_buffered(shape, index_map):
    return pl.BlockSpec(shape, index_map, pipeline_mode=pl.Buffered(1))


def _attn_a(p, batch, seq, tq=512, tk=2048):
    T = p.shape[0]
    group = N_HEADS_A // N_KV_A
    qw = group * HEAD_DIM
    tq = min(tq, seq)
    tk = min(tk, seq)
    nq = seq // tq
    unroll = 2 if (seq // tk) % 2 == 0 else 1
    return pl.pallas_call(
        functools.partial(_attn_a_kernel, tk=tk, unroll=unroll),
        grid=(batch, N_KV_A, nq),
        in_specs=[
            pl.BlockSpec((tq, qw), lambda b, h, i: (b * nq + i, COL_QA // qw + h)),
            _single_buffered((seq, HEAD_DIM), lambda b, h, i: (b, COL_KA // HEAD_DIM + h)),
            _single_buffered((seq, HEAD_DIM), lambda b, h, i: (b, COL_VA // HEAD_DIM + h)),
        ],
        out_specs=pl.BlockSpec((tq, qw), lambda b, h, i: (b * nq + i, h)),
        out_shape=jax.ShapeDtypeStruct((T, N_HEADS_A * HEAD_DIM), BF16),
        scratch_shapes=[pltpu.VMEM((group * tq, LANES), F32),
                        pltpu.VMEM((group * tq, 2 * HEAD_DIM), F32),
                        pltpu.VMEM((seq, 2 * HEAD_DIM), BF16)],
        compiler_params=_cparams(("parallel", "parallel", "arbitrary")),
        name="attn_a",
    )(p, p, p)


def _attn_b_kernel(lam_ref, q_ref, k_ref, v_ref, sub_ref, o_ref, m_ref, l_ref, acc_ref, *, tk, sub, unroll, lam_init):
    tq = q_ref.shape[0]
    m_ref[...] = jnp.full(m_ref.shape, NEG_BIG, F32)
    l_ref[...] = jnp.zeros(l_ref.shape, F32)
    acc_ref[...] = jnp.zeros(acc_ref.shape, F32)

    def body(i, carry):
        for u in range(unroll):
            off = pl.multiple_of((i * unroll + u) * tk, tk)
            v = v_ref[pl.ds(off, tk), :]
            for c in range(2):
                cols = slice(c * HEAD_DIM, (c + 1) * HEAD_DIM)
                k = k_ref[pl.ds(off, tk), cols]
                for r in range(tq // sub):
                    rows = pl.ds(c * tq + r * sub, sub)
                    s = _nt_dot(q_ref[r * sub:(r + 1) * sub, cols], k)
                    alpha, p = _softmax_block(s, m_ref.at[rows])
                    l_ref[rows, :] = alpha * l_ref[rows, :] + jnp.sum(p, axis=-1, keepdims=True)
                    pv = jnp.dot(p.astype(BF16), v, preferred_element_type=F32)
                    acc_ref[rows, :] = _lane_tile(alpha, 2) * acc_ref[rows, :] + pv
        return carry

    lax.fori_loop(0, k_ref.shape[0] // (tk * unroll), body, 0)
    lv = lam_ref[...]
    lam = (jnp.exp(jnp.sum(lv[0:1] * lv[1:2], axis=-1, keepdims=True))
           - jnp.exp(jnp.sum(lv[2:3] * lv[3:4], axis=-1, keepdims=True)) + lam_init)
    o = acc_ref[...] / _lane_tile(l_ref[...], 2)
    od = o[:tq] - lam * o[tq:]
    o_ref[...] = (_rms(od, sub_ref[...]) * (1.0 - lam_init)).astype(o_ref.dtype)


def _attn_b(p, lam_vecs, subln, batch, seq, lam_init, tq=1024, sub=256, tk=2048):
    T = p.shape[0]
    vw = 2 * HEAD_DIM
    tq = min(tq, seq)
    sub = min(sub, tq)
    tk = min(tk, seq)
    nq = seq // tq
    unroll = 2 if (seq // tk) % 2 == 0 else 1
    return pl.pallas_call(
        functools.partial(_attn_b_kernel, tk=tk, sub=sub, unroll=unroll, lam_init=lam_init),
        grid=(batch, N_HEADS_B, nq),
        in_specs=[
            pl.BlockSpec((4, HEAD_DIM), lambda b, h, i: (0, 0)),
            pl.BlockSpec((tq, vw), lambda b, h, i: (b * nq + i, COL_QB // vw + h)),
            _single_buffered((seq, vw), lambda b, h, i: (b, COL_KB // vw + h)),
            _single_buffered((seq, vw), lambda b, h, i: (b, COL_VB // vw + h)),
            pl.BlockSpec((1, vw), lambda b, h, i: (0, 0)),
        ],
        out_specs=pl.BlockSpec((tq, vw), lambda b, h, i: (b * nq + i, h)),
        out_shape=jax.ShapeDtypeStruct((T, N_HEADS_B * vw), BF16),
        scratch_shapes=[pltpu.VMEM((2 * tq, LANES), F32), pltpu.VMEM((2 * tq, LANES), F32),
                        pltpu.VMEM((2 * tq, vw), F32)],
        compiler_params=_cparams(("parallel", "parallel", "parallel")),
        name="attn_b",
    )(lam_vecs, p, p, p, subln)


def _merge_kernel(x_ref, oa_ref, ob_ref, ga_ref, gb_ref, wa_ref, wb_ref, wo_ref, o_ref):
    a = jnp.dot(oa_ref[...], wa_ref[...], preferred_element_type=F32)
    b = jnp.dot(ob_ref[...], wb_ref[...], preferred_element_type=F32)
    merged = ga_ref[...].astype(F32) * a + gb_ref[...].astype(F32) * b
    o_ref[...] = x_ref[...] + jnp.dot(merged.astype(BF16), wo_ref[...], preferred_element_type=F32)


def _resident(shape):
    return pl.BlockSpec(shape, lambda *_: (0,) * len(shape), pipeline_mode=pl.Buffered(1))


def _merge(x, oa, ob, p, wa, wb, wo, tm=256):
    T = x.shape[0]
    return pl.pallas_call(
        _merge_kernel,
        grid=(T // tm,),
        in_specs=[
            pl.BlockSpec((tm, D_MODEL), lambda i: (i, 0)),
            pl.BlockSpec((tm, oa.shape[1]), lambda i: (i, 0)),
            pl.BlockSpec((tm, ob.shape[1]), lambda i: (i, 0)),
            pl.BlockSpec((tm, D_MODEL), lambda i: (i, COL_GA // D_MODEL)),
            pl.BlockSpec((tm, D_MODEL), lambda i: (i, COL_GB // D_MODEL)),
            _resident(wa.shape), _resident(wb.shape), _resident(wo.shape),
        ],
        out_specs=pl.BlockSpec((tm, D_MODEL), lambda i: (i, 0)),
        out_shape=jax.ShapeDtypeStruct((T, D_MODEL), F32),
        compiler_params=_cparams(("parallel",)),
        name="merge",
    )(x, oa, ob, p, p, wa, wb, wo)


def _norm_matmul_kernel(x_ref, g_ref, w_ref, o_ref):
    xn = _rms(x_ref[...], g_ref[...]).astype(BF16)
    o_ref[...] = jnp.dot(xn, w_ref[...], preferred_element_type=F32).astype(o_ref.dtype)


def _norm_matmul(x, g, w, tm=256):
    T, K = x.shape
    N = w.shape[1]
    return pl.pallas_call(
        _norm_matmul_kernel,
        grid=(T // tm,),
        in_specs=[pl.BlockSpec((tm, K), lambda i: (i, 0)), pl.BlockSpec((1, K), lambda i: (0, 0)),
                  pl.BlockSpec((K, N), lambda i: (0, 0))],
        out_specs=pl.BlockSpec((tm, N), lambda i: (i, 0)),
        out_shape=jax.ShapeDtypeStruct((T, N), BF16),
        compiler_params=_cparams(("parallel",)),
        name="mem_kv",
    )(x, g, w)


def _cross_kernel(x_ref, g_ref, wq_ref, kv_ref, wo_ref, o_ref):
    x = x_ref[...]
    hn = _rms(x, g_ref[...]).astype(BF16)
    q = (jnp.dot(hn, wq_ref[...], preferred_element_type=F32) * QK_SCALE).astype(BF16)
    heads = []
    for h in range(N_HEADS_X):
        lo = h * HEAD_DIM
        s = _nt_dot(q[:, lo:lo + HEAD_DIM], kv_ref[:, lo:lo + HEAD_DIM])
        e = jnp.exp(s - jnp.max(s, axis=-1, keepdims=True))
        pr = (e / jnp.sum(e, axis=-1, keepdims=True)).astype(BF16)
        heads.append(jnp.dot(pr, kv_ref[:, WIDTH_X + lo:WIDTH_X + lo + HEAD_DIM], preferred_element_type=F32))
    o = jnp.concatenate(heads, axis=1).astype(BF16)
    o_ref[...] = x + jnp.dot(o, wo_ref[...], preferred_element_type=F32)


def _cross(x, g, wq, kv, wo, seq, tm=512):
    T = x.shape[0]
    tm = min(tm, seq)
    per_seq = seq // tm
    return pl.pallas_call(
        _cross_kernel,
        grid=(T // tm,),
        in_specs=[
            pl.BlockSpec((tm, D_MODEL), lambda i: (i, 0)),
            pl.BlockSpec((1, D_MODEL), lambda i: (0, 0)),
            pl.BlockSpec(wq.shape, lambda i: (0, 0)),
            pl.BlockSpec((MEM_LEN, 2 * WIDTH_X), lambda i: (i // per_seq, 0)),
            pl.BlockSpec(wo.shape, lambda i: (0, 0)),
        ],
        out_specs=pl.BlockSpec((tm, D_MODEL), lambda i: (i, 0)),
        out_shape=jax.ShapeDtypeStruct((T, D_MODEL), F32),
        compiler_params=_cparams(("parallel",)),
        name="cross",
    )(x, g, wq, kv, wo)


def _router_kernel(x_ref, g_ref, wr_ref, br_ref, cin_ref, hn_ref, ids_ref, wts_ref, rank_ref, cnt_ref, run_ref):
    i = pl.program_id(0)

    @pl.when(i == 0)
    def _():
        run_ref[...] = cin_ref[...]

    hn = _rms(x_ref[...], g_ref[...])
    hn_ref[...] = hn
    logits = jnp.dot(hn, wr_ref[...], preferred_element_type=F32, precision=lax.Precision.HIGHEST) + br_ref[...]
    tm = logits.shape[0]
    lane = lax.broadcasted_iota(jnp.int32, logits.shape, 1)
    lane_f = lane.astype(F32)
    vals = logits
    top_v, top_i, sels = [], [], []
    for _ in range(TOP_K):
        mk = jnp.max(vals, axis=-1, keepdims=True)
        idx = jnp.min(jnp.where(vals == mk, lane_f, float(LANES)), axis=-1, keepdims=True)
        sel = lane_f == idx
        top_v.append(mk)
        top_i.append(idx.astype(jnp.int32))
        sels.append(sel)
        vals = jnp.where(sel, 2.0 * NEG_BIG, vals)
    es = [jnp.exp(v - top_v[0]) for v in top_v]
    denom = es[0] + es[1] + es[2] + es[3]
    onehot = (sels[0] | sels[1] | sels[2] | sels[3]).astype(F32)
    r = lax.broadcasted_iota(jnp.int32, (tm, tm), 0)
    c = lax.broadcasted_iota(jnp.int32, (tm, tm), 1)
    tri = (c < r).astype(BF16)
    prefix = jnp.dot(tri, onehot.astype(BF16), preferred_element_type=F32)
    rank_te = prefix + run_ref[0:1, :]
    ids = jnp.zeros(logits.shape, jnp.int32)
    wts = jnp.zeros(logits.shape, F32)
    rank = jnp.zeros(logits.shape, jnp.int32)
    for k in range(TOP_K):
        rk = jnp.sum(jnp.where(sels[k], rank_te, 0.0), axis=-1, keepdims=True).astype(jnp.int32)
        ids = jnp.where(lane == k, top_i[k], ids)
        wts = jnp.where(lane == k, es[k] / denom, wts)
        rank = jnp.where(lane == k, rk, rank)
    ids_ref[...] = ids
    wts_ref[...] = wts
    rank_ref[...] = rank
    run_ref[...] = run_ref[...] + jnp.sum(onehot, axis=0, keepdims=True)
    cnt_ref[...] = run_ref[...]


def _router(x, g, wr, br, counts_in, tm=512):
    T = x.shape[0]
    row = pl.BlockSpec((tm, LANES), lambda i: (i, 0))
    return pl.pallas_call(
        _router_kernel,
        grid=(T // tm,),
        in_specs=[pl.BlockSpec((tm, D_MODEL), lambda i: (i, 0)), pl.BlockSpec((1, D_MODEL), lambda i: (0, 0)),
                  pl.BlockSpec((D_MODEL, LANES), lambda i: (0, 0)), pl.BlockSpec((1, LANES), lambda i: (0, 0)),
                  pl.BlockSpec((8, LANES), lambda i: (0, 0))],
        out_specs=[pl.BlockSpec((tm, D_MODEL), lambda i: (i, 0)), row, row, row,
                   pl.BlockSpec((8, LANES), lambda i: (0, 0))],
        out_shape=[jax.ShapeDtypeStruct((T, D_MODEL), F32), jax.ShapeDtypeStruct((T, LANES), jnp.int32),
                   jax.ShapeDtypeStruct((T, LANES), F32), jax.ShapeDtypeStruct((T, LANES), jnp.int32),
                   jax.ShapeDtypeStruct((8, LANES), F32)],
        scratch_shapes=[pltpu.VMEM((8, LANES), F32)],
        compiler_params=_cparams(("arbitrary",)),
        name="router",
    )(x, g, wr, br, counts_in)


DISPATCH_CHUNK = 512


ISSUE_UNROLL = 2


def _dispatch_kernel(slot_ref, hn_ref, xs_in_ref, xs_ref, sem):
    del xs_in_ref

    def issue(j, carry):
        for u in range(ISSUE_UNROLL):
            c = j * ISSUE_UNROLL + u
            for k in range(TOP_K):
                slot = slot_ref[0, 0, c * TOP_K + k]
                pltpu.make_async_copy(hn_ref.at[pl.ds(c, 1)], xs_ref.at[pl.ds(slot, 1)], sem).start()
        return carry

    lax.fori_loop(0, DISPATCH_CHUNK // ISSUE_UNROLL, issue, 0)
    for k in range(TOP_K):
        pltpu.make_async_copy(hn_ref, xs_ref.at[pl.ds(0, DISPATCH_CHUNK)], sem).wait()


def _zero_tail_kernel(tail_ref, o_ref):
    del tail_ref
    o_ref[...] = jnp.zeros(o_ref.shape, F32)


def _slot_buffer(tail_tile, n_slots):
    return pl.pallas_call(
        _zero_tail_kernel,
        grid_spec=pltpu.PrefetchScalarGridSpec(
            num_scalar_prefetch=1,
            grid=(tail_tile.shape[0],),
            in_specs=[],
            out_specs=pl.BlockSpec((MOE_TM, D_MODEL), lambda e, tail: (tail[e], 0)),
        ),
        out_shape=jax.ShapeDtypeStruct((n_slots, D_MODEL), F32),
        compiler_params=_cparams(("arbitrary",)),
        name="zero_tail",
    )(tail_tile)


def _dispatch(hn, slot4, xs0):
    T = hn.shape[0]
    n = T // DISPATCH_CHUNK
    n_slots = xs0.shape[0]
    idx_spec = pl.BlockSpec((1, 1, DISPATCH_CHUNK * TOP_K), lambda i: (i, 0, 0), memory_space=pltpu.SMEM)
    return pl.pallas_call(
        _dispatch_kernel,
        grid=(n,),
        in_specs=[idx_spec, pl.BlockSpec((DISPATCH_CHUNK, D_MODEL), lambda i: (i, 0)),
                  pl.BlockSpec(memory_space=pl.ANY)],
        out_specs=pl.BlockSpec(memory_space=pl.ANY),
        scratch_shapes=[pltpu.SemaphoreType.DMA(())],
        out_shape=jax.ShapeDtypeStruct((n_slots, D_MODEL), F32),
        input_output_aliases={2: 0},
        compiler_params=_cparams(("arbitrary",)),
        name="dispatch",
    )(slot4.reshape(n, 1, -1), hn, xs0)


SPLIT_TN = 512


def _split_glu_kernel(w_ref, perm_ref, g_ref, l_ref):
    r = jnp.dot(w_ref[0].astype(BF16), perm_ref[...], preferred_element_type=F32)
    half = SPLIT_TN // 2
    g_ref[0] = r[:, :half].astype(BF16)
    l_ref[0] = r[:, half:].astype(BF16)


def _split_glu(w1):
    E, K, N2 = w1.shape
    half = SPLIT_TN // 2
    src = jnp.arange(SPLIT_TN, dtype=jnp.int32)
    dst = jnp.where(src % 2 == 0, src // 2, half + src // 2)
    perm = (dst[:, None] == jnp.arange(SPLIT_TN, dtype=jnp.int32)[None, :]).astype(BF16)
    out = jax.ShapeDtypeStruct((E, K, N2 // 2), BF16)
    return pl.pallas_call(
        _split_glu_kernel,
        grid=(E, N2 // SPLIT_TN),
        in_specs=[pl.BlockSpec((1, K, SPLIT_TN), lambda e, j: (e, 0, j)),
                  pl.BlockSpec((SPLIT_TN, SPLIT_TN), lambda e, j: (0, 0))],
        out_specs=[pl.BlockSpec((1, K, half), lambda e, j: (e, 0, j)),
                   pl.BlockSpec((1, K, half), lambda e, j: (e, 0, j))],
        out_shape=[out, out],
        compiler_params=_cparams(("parallel", "parallel")),
        name="split_glu",
    )(w1, perm)


MOE_TM = 512
MOE_TF = 1024
MOE_SUB = 512


MOE_GROUP = 1


def _experts_kernel(ge_ref, tv_ref, xs_ref, w1g_ref, w1l_ref, b1g_ref, b1l_ref, w2_ref, b2_ref, ys_ref, xb_ref, acc_ref):
    i = pl.program_id(0)
    f = pl.program_id(1)
    s = pl.program_id(2)
    last = pl.num_programs(1) - 1
    valid = tv_ref[i * MOE_GROUP + s] > 0

    @pl.when(valid)
    def _():
        @pl.when(f == 0)
        def _():
            xb_ref[s] = xs_ref[...].astype(BF16)

        xb = xb_ref[s]
        part = None
        for c in range(MOE_TF // MOE_SUB):
            cols = slice(c * MOE_SUB, (c + 1) * MOE_SUB)
            glu = jnp.dot(xb, w1g_ref[0, :, cols], preferred_element_type=F32) + b1g_ref[0, :, cols]
            lin = jnp.dot(xb, w1l_ref[0, :, cols], preferred_element_type=F32) + b1l_ref[0, :, cols]
            glu = jnp.minimum(glu, SWIGLU_LIMIT)
            lin = jnp.clip(lin, -SWIGLU_LIMIT, SWIGLU_LIMIT)
            act = glu * jax.nn.sigmoid(SWIGLU_ALPHA * glu) * (lin + 1.0)
            sub_part = jnp.dot(act.astype(BF16), w2_ref[0, cols, :], preferred_element_type=F32)
            part = sub_part if part is None else part + sub_part

        @pl.when(f == 0)
        def _():
            acc_ref[s] = part

        @pl.when((f > 0) & (f < last))
        def _():
            acc_ref[s] += part

        @pl.when(f == last)
        def _():
            ys_ref[...] = acc_ref[s] + part + b2_ref[0]

    @pl.when(jnp.logical_not(valid) & (f == last))
    def _():
        ys_ref[...] = jnp.zeros(ys_ref.shape, F32)


def _experts(xs, group_expert, tile_valid, w1g, w1l, b1g, b1l, w2, b2):
    n_slots = xs.shape[0]
    n_groups = n_slots // (MOE_TM * MOE_GROUP)
    nf = D_FF // MOE_TF
    assert nf >= 2
    G = MOE_GROUP

    def xs_map(i, f, s, ge, tv):
        return (jnp.where(f == 0, i * G + s, i * G + G - 1), 0)

    def ys_map(i, f, s, ge, tv):
        return (jnp.where(f == nf - 1, i * G + s, i * G), 0)

    return pl.pallas_call(
        _experts_kernel,
        grid_spec=pltpu.PrefetchScalarGridSpec(
            num_scalar_prefetch=2,
            grid=(n_groups, nf, G),
            in_specs=[
                pl.BlockSpec((MOE_TM, D_MODEL), xs_map),
                pl.BlockSpec((1, D_MODEL, MOE_TF), lambda i, f, s, ge, tv: (ge[i], 0, f)),
                pl.BlockSpec((1, D_MODEL, MOE_TF), lambda i, f, s, ge, tv: (ge[i], 0, f)),
                pl.BlockSpec((1, 1, MOE_TF), lambda i, f, s, ge, tv: (ge[i], 0, f)),
                pl.BlockSpec((1, 1, MOE_TF), lambda i, f, s, ge, tv: (ge[i], 0, f)),
                pl.BlockSpec((1, MOE_TF, D_MODEL), lambda i, f, s, ge, tv: (ge[i], f, 0)),
                pl.BlockSpec((1, 1, D_MODEL), lambda i, f, s, ge, tv: (ge[i], 0, 0)),
            ],
            out_specs=pl.BlockSpec((MOE_TM, D_MODEL), ys_map),
            scratch_shapes=[pltpu.VMEM((G, MOE_TM, D_MODEL), BF16), pltpu.VMEM((G, MOE_TM, D_MODEL), F32)],
        ),
        out_shape=jax.ShapeDtypeStruct((n_slots, D_MODEL), F32),
        compiler_params=_cparams(("parallel", "arbitrary", "arbitrary")),
        name="experts",
    )(group_expert, tile_valid, xs, w1g, w1l, b1g, b1l, w2, b2)


COMBINE_CHUNK = 128


def _combine_kernel(slot_ref, slot_nxt_ref, x_ref, wts_ref, g_ref, ys_ref, o_ref, buf_ref, sem):
    i = pl.program_id(0)
    n = pl.num_programs(0)
    cur = i % 2

    def gather(slots, buf):
        def issue(j, carry):
            for u in range(ISSUE_UNROLL):
                c = j * ISSUE_UNROLL + u
                for k in range(TOP_K):
                    slot = slots[0, 0, c * TOP_K + k]
                    pltpu.make_async_copy(ys_ref.at[pl.ds(slot, 1)], buf_ref.at[buf, k, pl.ds(c, 1)],
                                          sem.at[buf]).start()
            return carry
        lax.fori_loop(0, COMBINE_CHUNK // ISSUE_UNROLL, issue, 0)

    @pl.when(i == 0)
    def _():
        gather(slot_ref, 0)

    @pl.when(i + 1 < n)
    def _():
        gather(slot_nxt_ref, 1 - cur)

    for k in range(TOP_K):
        pltpu.make_async_copy(ys_ref.at[pl.ds(0, COMBINE_CHUNK)], buf_ref.at[cur, k], sem.at[cur]).wait()
    wts = wts_ref[...]
    y = x_ref[...]
    for k in range(TOP_K):
        y = y + wts[:, k:k + 1] * buf_ref[cur, k]
    o_ref[...] = _rms(y, g_ref[...])


def _combine(x, wts, g, ys, slot4):
    T = x.shape[0]
    n = T // COMBINE_CHUNK
    idx_shape = (1, 1, COMBINE_CHUNK * TOP_K)
    idx_spec = pl.BlockSpec(idx_shape, lambda i: (i, 0, 0), memory_space=pltpu.SMEM)
    nxt_spec = pl.BlockSpec(idx_shape, lambda i: (jnp.minimum(i + 1, n - 1), 0, 0), memory_space=pltpu.SMEM)
    slots = slot4.reshape(n, 1, -1)
    return pl.pallas_call(
        _combine_kernel,
        grid=(n,),
        in_specs=[idx_spec, nxt_spec,
                  pl.BlockSpec((COMBINE_CHUNK, D_MODEL), lambda i: (i, 0)),
                  pl.BlockSpec((COMBINE_CHUNK, LANES), lambda i: (i, 0)),
                  pl.BlockSpec((1, D_MODEL), lambda i: (0, 0)),
                  pl.BlockSpec(memory_space=pl.ANY)],
        out_specs=pl.BlockSpec((COMBINE_CHUNK, D_MODEL), lambda i: (i, 0)),
        scratch_shapes=[pltpu.VMEM((2, TOP_K, COMBINE_CHUNK, D_MODEL), F32), pltpu.SemaphoreType.DMA((2,))],
        out_shape=jax.ShapeDtypeStruct((T, D_MODEL), F32),
        compiler_params=_cparams(("arbitrary",)),
        name="combine",
    )(slots, slots, x, wts, g, ys)


def _mix_and_cross(x3, mem3, w, lam_init):
    batch, seq, _ = x3.shape
    x = x3.reshape(batch * seq, D_MODEL)
    mem = mem3.reshape(batch * MEM_LEN, D_MODEL)
    p = _proj(x, w["norm_mix"], w["w_in"], _rope_tables(seq), w["qk_gain"], seq)
    oa = _attn_a(p, batch, seq)
    ob = _attn_b(p, w["lam_vecs"], w["subln_b"], batch, seq, lam_init)
    x = _merge(x, oa, ob, p, w["w_branch_a"], w["w_branch_b"], w["w_out"])
    kv = _norm_matmul(mem, w["norm_mem"], w["w_ckv"])
    return _cross(x, w["norm_cross"], w["w_cq"], kv, w["w_co"], seq)


def _moe_and_final_norm(xs_list, w):
    counts = jnp.zeros((8, LANES), F32)
    routed = []
    for x in xs_list:
        hn, ids, gate_w, rank, counts = _router(x, w["norm_moe"], w["w_router"], w["b_router"], counts)
        routed.append((hn, ids[:, :TOP_K], rank[:, :TOP_K], gate_w))
    rows_per_group = MOE_TM * MOE_GROUP
    cnt = counts[0, :N_EXPERTS].astype(jnp.int32)
    padded = ((cnt + rows_per_group - 1) // rows_per_group) * rows_per_group
    group_end = jnp.cumsum(padded)
    group_start = group_end - padded
    total = sum(x.shape[0] for x in xs_list) * TOP_K
    n_groups = total // rows_per_group + N_EXPERTS
    lo = jnp.arange(n_groups, dtype=jnp.int32) * rows_per_group
    group_expert = jnp.sum((lo[:, None] >= group_end[None, :]).astype(jnp.int32), axis=1)
    last_expert = jnp.max(jnp.where(cnt > 0, jnp.arange(N_EXPERTS, dtype=jnp.int32), 0))
    group_expert = jnp.minimum(group_expert, last_expert).astype(jnp.int32)
    tile_lo = jnp.arange(n_groups * MOE_GROUP, dtype=jnp.int32) * MOE_TM
    tile_e = jnp.repeat(group_expert, MOE_GROUP)
    tile_valid = (tile_lo < group_start[tile_e] + cnt[tile_e]).astype(jnp.int32)
    slot4s = [group_start[ids4] + rank4 for _, ids4, rank4, _ in routed]
    n_tiles = n_groups * MOE_GROUP
    assert MOE_GROUP == 1, "pad rows must be confined to each expert's last row tile"
    tail_tile = jnp.where(cnt > 0, group_end // MOE_TM - 1, n_tiles - 1).astype(jnp.int32)
    xs = _slot_buffer(tail_tile, n_tiles * MOE_TM)
    for (hn, _, _, _), slot4 in zip(routed, slot4s):
        xs = _dispatch(hn, slot4, xs)
    ys = _experts(xs, group_expert, tile_valid, w["w1g"], w["w1l"], w["b1g"], w["b1l"], w["w2"], w["b2"])
    return [_combine(x, gate_w, w["norm_final"], ys, slot4)
            for x, (_, _, _, gate_w), slot4 in zip(xs_list, routed, slot4s)]


def kernel(x_prompt, x_sample, mem_prompt, mem_sample, norm_mix, w_in, q_norm_a, k_norm_a, lambda_q1, lambda_k1, lambda_q2, lambda_k2, subln_b, w_branch_a, w_branch_b, w_out, norm_cross, norm_mem, w_cq, w_ckv, w_co, norm_moe, w_router, b_router, w_exp1, b_exp1, w_exp2, b_exp2, norm_final):
    assert DEPTH == 1
    wts = []
    for l in range(DEPTH):
        wr = jnp.zeros((D_MODEL, LANES), F32).at[:, :N_EXPERTS].set(w_router[l])
        br = jnp.full((1, LANES), NEG_BIG, F32).at[0, :N_EXPERTS].set(b_router[l])
        w1g, w1l = _split_glu(w_exp1[l])
        wts.append(dict(
            norm_mix=norm_mix[l][None, :],
            w_in=jnp.concatenate([w_in[l][:, REF_GATE_START:], w_in[l][:, :REF_GATE_START]], axis=1).astype(BF16),
            qk_gain=jnp.stack([q_norm_a[l], k_norm_a[l]]),
            lam_vecs=jnp.stack([lambda_q1[l], lambda_k1[l], lambda_q2[l], lambda_k2[l]]),
            subln_b=subln_b[l][None, :],
            w_branch_a=w_branch_a[l].astype(BF16),
            w_branch_b=w_branch_b[l].astype(BF16),
            w_out=w_out[l].astype(BF16),
            norm_cross=norm_cross[l][None, :],
            norm_mem=norm_mem[l][None, :],
            w_cq=w_cq[l].astype(BF16),
            w_ckv=w_ckv[l].astype(BF16),
            w_co=w_co[l].astype(BF16),
            norm_moe=norm_moe[l][None, :],
            w_router=wr,
            b_router=br,
            w1g=w1g,
            w1l=w1l,
            b1g=b_exp1[l][:, None, 0::2],
            b1l=b_exp1[l][:, None, 1::2],
            w2=w_exp2[l].astype(BF16),
            b2=b_exp2[l][:, None, :],
            norm_final=norm_final[None, :],
        ))
    w = wts[0]
    lam_init = 0.8 - 0.6 * math.exp(-0.3 * 0)
    streams = [_mix_and_cross(x_prompt, mem_prompt, w, lam_init), _mix_and_cross(x_sample, mem_sample, w, lam_init)]
    y_prompt, y_sample = _moe_and_final_norm(streams, w)
    return (y_prompt.reshape(x_prompt.shape), y_sample.reshape(x_sample.shape))
```

```python
import functools
import math

import jax
import jax.numpy as jnp
from jax import lax
from jax.experimental import pallas as pl
from jax.experimental.pallas import tpu as pltpu

F32 = jnp.float32
BF16 = jnp.bfloat16

D_MODEL = 2048
DEPTH = 1
GRID_W = 64
RMS_EPS = 1e-6
N_HEADS_A = 8
N_KV_A = 2
HEAD_DIM = 128
AXIAL_THETA = 10000.0
N_HEADS_B = 4
ROPE_THETA = 500000.0
ROT_DIM_B = HEAD_DIM // 4
MEM_LEN = 256
N_HEADS_X = 4
WIDTH_X = N_HEADS_X * HEAD_DIM
N_EXPERTS = 32
TOP_K = 4
D_FF = D_MODEL
SWIGLU_ALPHA = 1.702
SWIGLU_LIMIT = 7.0
LANES = 128
IN_COLS = 8704
QK_SCALE = HEAD_DIM ** -0.5
QK_SCALE_LOG2E = QK_SCALE * math.log2(math.e)
NEG_BIG = -1e30

VMEM_LIMIT = 56 * 1024 * 1024

COL_GA = 0
COL_GB = 2048
COL_QA = 4096
COL_KA = 5120
COL_VA = 5376
COL_QB = 5632
COL_KB = 6656
COL_VB = 7680
REF_GATE_START = 4608


def _cparams(sem):
    return pltpu.CompilerParams(dimension_semantics=sem, vmem_limit_bytes=VMEM_LIMIT)


def _rms(x, g):
    return x * lax.rsqrt(jnp.mean(x * x, axis=-1, keepdims=True) + RMS_EPS) * g


def _rotate(x, cos, sin_signed, shift):
    n = x.shape[-1]
    lane = lax.broadcasted_iota(jnp.int32, x.shape, 1)
    up = pltpu.roll(x, n - shift, 1)
    dn = pltpu.roll(x, shift, 1)
    partner = jnp.where((lane % (2 * shift)) < shift, up, dn)
    return x * cos + partner * sin_signed


PROJ_TN = 512


def _proj_kernel(x_ref, g_ref, w_ref, ca_ref, sa_ref, cb_ref, sb_ref, qk_ref, o_ref, xn_ref):
    j = pl.program_id(1)

    @pl.when(j == 0)
    def _():
        xn_ref[...] = _rms(x_ref[...], g_ref[...]).astype(BF16)

    tm = xn_ref.shape[0]
    n_sub = 2 if tm % 16 == 0 else 1
    sub = tm // n_sub

    def axial(gain, scale):
        def f(a, rows):
            r = _rotate(_rms(a, gain), ca_ref[rows, :], sa_ref[rows, :], HEAD_DIM // 4)
            return r * scale if scale != 1.0 else r
        return f

    def partial_rot(scale):
        def f(a, rows):
            r = _rotate(a, cb_ref[rows, :], sb_ref[rows, :], ROT_DIM_B // 2)
            return r * scale if scale != 1.0 else r
        return f

    def plain(a, rows):
        return a

    def gate(a, rows):
        return jax.nn.sigmoid(a)

    def run(epilogues):
        for r in range(n_sub):
            rows = slice(r * sub, (r + 1) * sub)
            acc = jnp.dot(xn_ref[rows, :], w_ref[...], preferred_element_type=F32)
            for c, fn in enumerate(epilogues):
                cols = slice(c * LANES, (c + 1) * LANES)
                o_ref[rows, cols] = fn(acc[:, cols], rows).astype(o_ref.dtype)

    n_chunks = PROJ_TN // LANES
    q_gain, k_gain = qk_ref[0:1, :], qk_ref[1:2, :]

    @pl.when(j < 8)
    def _():
        run([gate] * n_chunks)

    @pl.when((j >= 8) & (j < 10))
    def _():
        run([axial(q_gain, QK_SCALE_LOG2E)] * n_chunks)

    @pl.when(j == 10)
    def _():
        run([axial(k_gain, 1.0), axial(k_gain, 1.0), plain, plain])

    @pl.when((j >= 11) & (j < 13))
    def _():
        run([partial_rot(QK_SCALE_LOG2E)] * n_chunks)

    @pl.when((j >= 13) & (j < 15))
    def _():
        run([partial_rot(1.0)] * n_chunks)

    @pl.when(j >= 15)
    def _():
        run([plain] * n_chunks)


def _rope_tables(seq):
    t = jnp.arange(seq, dtype=jnp.int32)
    row = (t // GRID_W).astype(F32)
    col = (t % GRID_W).astype(F32)
    lane = jnp.arange(LANES, dtype=jnp.int32)
    half = HEAD_DIM // 4
    inv_a = jnp.power(AXIAL_THETA, -2.0 * (lane % half).astype(F32) / (HEAD_DIM // 2))
    pos_a = jnp.where(lane[None, :] < HEAD_DIM // 2, row[:, None], col[:, None])
    ang_a = pos_a * inv_a[None, :]
    sign_a = jnp.where((lane % (2 * half)) < half, -1.0, 1.0).astype(F32)
    ca, sa = jnp.cos(ang_a), jnp.sin(ang_a) * sign_a[None, :]
    hb = ROT_DIM_B // 2
    inv_b = jnp.power(ROPE_THETA, -2.0 * (lane % hb).astype(F32) / ROT_DIM_B)
    ang_b = t.astype(F32)[:, None] * inv_b[None, :]
    rot = (lane < ROT_DIM_B)[None, :]
    sign_b = jnp.where((lane % (2 * hb)) < hb, -1.0, 1.0).astype(F32)
    cb = jnp.where(rot, jnp.cos(ang_b), 1.0)
    sb = jnp.where(rot, jnp.sin(ang_b) * sign_b[None, :], 0.0)
    return ca, sa, cb, sb


def _proj(x, g, w, tables, qk_gain, seq, tm=1024):
    T = x.shape[0]
    tm = min(tm, seq)
    nseq = seq // tm
    tab_spec = pl.BlockSpec((tm, LANES), lambda i, j: (i % nseq, 0))
    return pl.pallas_call(
        _proj_kernel,
        grid=(T // tm, IN_COLS // PROJ_TN),
        in_specs=[
            pl.BlockSpec((tm, D_MODEL), lambda i, j: (i, 0)),
            pl.BlockSpec((1, D_MODEL), lambda i, j: (0, 0)),
            pl.BlockSpec((D_MODEL, PROJ_TN), lambda i, j: (0, j)),
            tab_spec, tab_spec, tab_spec, tab_spec,
            pl.BlockSpec((2, LANES), lambda i, j: (0, 0)),
        ],
        out_specs=pl.BlockSpec((tm, PROJ_TN), lambda i, j: (i, j)),
        out_shape=jax.ShapeDtypeStruct((T, IN_COLS), BF16),
        scratch_shapes=[pltpu.VMEM((tm, D_MODEL), BF16)],
        compiler_params=_cparams(("parallel", "arbitrary")),
        name="proj",
    )(x, g, w, *tables, qk_gain)


def _nt_dot(a, b):
    return lax.dot_general(a, b, (((1,), (1,)), ((), ())), preferred_element_type=F32)


def _lane_tile(x, n):
    return jnp.concatenate([x] * n, axis=1)


def _softmax_block(s, m_ref):
    m_prev = m_ref[...]
    m_new = jnp.maximum(m_prev, jnp.max(s, axis=-1, keepdims=True))
    alpha = jnp.exp2(m_prev - m_new)
    p = jnp.exp2(s - _lane_tile(m_new, s.shape[1] // LANES))
    m_ref[...] = m_new
    return alpha, p


def _attn_a_kernel(q_ref, k_ref, v_ref, o_ref, m_ref, acc_ref, vaug_ref, *, tk, unroll):
    tq = q_ref.shape[0]
    group = N_HEADS_A // N_KV_A

    @pl.when(pl.program_id(2) == 0)
    def _():
        vaug_ref[:, :HEAD_DIM] = v_ref[...]
        vaug_ref[:, HEAD_DIM:] = jnp.ones(v_ref.shape, BF16)

    m_ref[...] = jnp.full(m_ref.shape, NEG_BIG, F32)
    acc_ref[...] = jnp.zeros(acc_ref.shape, F32)

    def body(i, carry):
        for u in range(unroll):
            off = pl.multiple_of((i * unroll + u) * tk, tk)
            k = k_ref[pl.ds(off, tk), :]
            v = vaug_ref[pl.ds(off, tk), :]
            for g in range(group):
                rows = pl.ds(g * tq, tq)
                s = _nt_dot(q_ref[:, g * HEAD_DIM:(g + 1) * HEAD_DIM], k)
                alpha, p = _softmax_block(s, m_ref.at[rows])
                pv = jnp.dot(p.astype(BF16), v, preferred_element_type=F32)
                acc_ref[rows, :] = _lane_tile(alpha, 2) * acc_ref[rows, :] + pv
        return carry

    lax.fori_loop(0, k_ref.shape[0] // (tk * unroll), body, 0)
    o = acc_ref[:, :HEAD_DIM] / acc_ref[:, HEAD_DIM:]
    for g in range(group):
        o_ref[:, g * HEAD_DIM:(g + 1) * HEAD_DIM] = o[g * tq:(g + 1) * tq].astype(o_ref.dtype)


def _single_buffered(shape, index_map):
    return pl.BlockSpec(shape, index_map, pipeline_mode=pl.Buffered(1))


def _attn_a(p, batch, seq, tq=512, tk=2048):
    T = p.shape[0]
    group = N_HEADS_A // N_KV_A
    qw = group * HEAD_DIM
    tq = min(tq, seq)
    tk = min(tk, seq)
    nq = seq // tq
    unroll = max(u for u in (1, 2, 4) if (seq // tk) % u == 0)
    return pl.pallas_call(
        functools.partial(_attn_a_kernel, tk=tk, unroll=unroll),
        grid=(batch, N_KV_A, nq),
        in_specs=[
            pl.BlockSpec((tq, qw), lambda b, h, i: (b * nq + i, COL_QA // qw + h)),
            _single_buffered((seq, HEAD_DIM), lambda b, h, i: (b, COL_KA // HEAD_DIM + h)),
            _single_buffered((seq, HEAD_DIM), lambda b, h, i: (b, COL_VA // HEAD_DIM + h)),
        ],
        out_specs=pl.BlockSpec((tq, qw), lambda b, h, i: (b * nq + i, h)),
        out_shape=jax.ShapeDtypeStruct((T, N_HEADS_A * HEAD_DIM), BF16),
        scratch_shapes=[pltpu.VMEM((group * tq, LANES), F32),
                        pltpu.VMEM((group * tq, 2 * HEAD_DIM), F32),
                        pltpu.VMEM((seq, 2 * HEAD_DIM), BF16)],
        compiler_params=_cparams(("parallel", "parallel", "arbitrary")),
        name="attn_a",
    )(p, p, p)


def _attn_b_kernel(lam_ref, q_ref, k_ref, v_ref, sub_ref, o_ref, m_ref, l_ref, acc_ref, *, tk, sub, unroll, lam_init):
    tq = q_ref.shape[0]
    m_ref[...] = jnp.full(m_ref.shape, NEG_BIG, F32)
    l_ref[...] = jnp.zeros(l_ref.shape, F32)
    acc_ref[...] = jnp.zeros(acc_ref.shape, F32)

    def body(i, carry):
        for u in range(unroll):
            off = pl.multiple_of((i * unroll + u) * tk, tk)
            v = v_ref[pl.ds(off, tk), :]
            for c in range(2):
                cols = slice(c * HEAD_DIM, (c + 1) * HEAD_DIM)
                k = k_ref[pl.ds(off, tk), cols]
                for r in range(tq // sub):
                    rows = pl.ds(c * tq + r * sub, sub)
                    s = _nt_dot(q_ref[r * sub:(r + 1) * sub, cols], k)
                    alpha, p = _softmax_block(s, m_ref.at[rows])
                    l_ref[rows, :] = alpha * l_ref[rows, :] + jnp.sum(p, axis=-1, keepdims=True)
                    pv = jnp.dot(p.astype(BF16), v, preferred_element_type=F32)
                    acc_ref[rows, :] = _lane_tile(alpha, 2) * acc_ref[rows, :] + pv
        return carry

    lax.fori_loop(0, k_ref.shape[0] // (tk * unroll), body, 0)
    lv = lam_ref[...]
    lam = (jnp.exp(jnp.sum(lv[0:1] * lv[1:2], axis=-1, keepdims=True))
           - jnp.exp(jnp.sum(lv[2:3] * lv[3:4], axis=-1, keepdims=True)) + lam_init)
    o = acc_ref[...] / _lane_tile(l_ref[...], 2)
    od = o[:tq] - lam * o[tq:]
    o_ref[...] = (_rms(od, sub_ref[...]) * (1.0 - lam_init)).astype(o_ref.dtype)


def _attn_b(p, lam_vecs, subln, batch, seq, lam_init, tq=1024, sub=256, tk=2048):
    T = p.shape[0]
    vw = 2 * HEAD_DIM
    tq = min(tq, seq)
    sub = min(sub, tq)
    tk = min(tk, seq)
    nq = seq // tq
    unroll = 2 if (seq // tk) % 2 == 0 else 1
    return pl.pallas_call(
        functools.partial(_attn_b_kernel, tk=tk, sub=sub, unroll=unroll, lam_init=lam_init),
        grid=(batch, N_HEADS_B, nq),
        in_specs=[
            pl.BlockSpec((4, HEAD_DIM), lambda b, h, i: (0, 0)),
            pl.BlockSpec((tq, vw), lambda b, h, i: (b * nq + i, COL_QB // vw + h)),
            _single_buffered((seq, vw), lambda b, h, i: (b, COL_KB // vw + h)),
            _single_buffered((seq, vw), lambda b, h, i: (b, COL_VB // vw + h)),
            pl.BlockSpec((1, vw), lambda b, h, i: (0, 0)),
        ],
        out_specs=pl.BlockSpec((tq, vw), lambda b, h, i: (b * nq + i, h)),
        out_shape=jax.ShapeDtypeStruct((T, N_HEADS_B * vw), BF16),
        scratch_shapes=[pltpu.VMEM((2 * tq, LANES), F32), pltpu.VMEM((2 * tq, LANES), F32),
                        pltpu.VMEM((2 * tq, vw), F32)],
        compiler_params=_cparams(("parallel", "parallel", "parallel")),
        name="attn_b",
    )(lam_vecs, p, p, p, subln)


def _merge_kernel(x_ref, oa_ref, ob_ref, ga_ref, gb_ref, wa_ref, wb_ref, wo_ref, o_ref):
    a = jnp.dot(oa_ref[...], wa_ref[...], preferred_element_type=F32)
    b = jnp.dot(ob_ref[...], wb_ref[...], preferred_element_type=F32)
    merged = ga_ref[...].astype(F32) * a + gb_ref[...].astype(F32) * b
    o_ref[...] = x_ref[...] + jnp.dot(merged.astype(BF16), wo_ref[...], preferred_element_type=F32)


def _resident(shape):
    return pl.BlockSpec(shape, lambda *_: (0,) * len(shape), pipeline_mode=pl.Buffered(1))


def _merge(x, oa, ob, p, wa, wb, wo, tm=256):
    T = x.shape[0]
    return pl.pallas_call(
        _merge_kernel,
        grid=(T // tm,),
        in_specs=[
            pl.BlockSpec((tm, D_MODEL), lambda i: (i, 0)),
            pl.BlockSpec((tm, oa.shape[1]), lambda i: (i, 0)),
            pl.BlockSpec((tm, ob.shape[1]), lambda i: (i, 0)),
            pl.BlockSpec((tm, D_MODEL), lambda i: (i, COL_GA // D_MODEL)),
            pl.BlockSpec((tm, D_MODEL), lambda i: (i, COL_GB // D_MODEL)),
            _resident(wa.shape), _resident(wb.shape), _resident(wo.shape),
        ],
        out_specs=pl.BlockSpec((tm, D_MODEL), lambda i: (i, 0)),
        out_shape=jax.ShapeDtypeStruct((T, D_MODEL), F32),
        compiler_params=_cparams(("parallel",)),
        name="merge",
    )(x, oa, ob, p, p, wa, wb, wo)


def _norm_matmul_kernel(x_ref, g_ref, w_ref, o_ref):
    xn = _rms(x_ref[...], g_ref[...]).astype(BF16)
    o_ref[...] = jnp.dot(xn, w_ref[...], preferred_element_type=F32).astype(o_ref.dtype)


def _norm_matmul(x, g, w, tm=256):
    T, K = x.shape
    N = w.shape[1]
    return pl.pallas_call(
        _norm_matmul_kernel,
        grid=(T // tm,),
        in_specs=[pl.BlockSpec((tm, K), lambda i: (i, 0)), pl.BlockSpec((1, K), lambda i: (0, 0)),
                  pl.BlockSpec((K, N), lambda i: (0, 0))],
        out_specs=pl.BlockSpec((tm, N), lambda i: (i, 0)),
        out_shape=jax.ShapeDtypeStruct((T, N), BF16),
        compiler_params=_cparams(("parallel",)),
        name="mem_kv",
    )(x, g, w)


def _cross_kernel(x_ref, g_ref, wq_ref, kv_ref, wo_ref, o_ref):
    x = x_ref[...]
    hn = _rms(x, g_ref[...]).astype(BF16)
    q = (jnp.dot(hn, wq_ref[...], preferred_element_type=F32) * QK_SCALE).astype(BF16)
    heads = []
    for h in range(N_HEADS_X):
        lo = h * HEAD_DIM
        s = _nt_dot(q[:, lo:lo + HEAD_DIM], kv_ref[:, lo:lo + HEAD_DIM])
        e = jnp.exp(s - jnp.max(s, axis=-1, keepdims=True))
        pr = (e / jnp.sum(e, axis=-1, keepdims=True)).astype(BF16)
        heads.append(jnp.dot(pr, kv_ref[:, WIDTH_X + lo:WIDTH_X + lo + HEAD_DIM], preferred_element_type=F32))
    o = jnp.concatenate(heads, axis=1).astype(BF16)
    o_ref[...] = x + jnp.dot(o, wo_ref[...], preferred_element_type=F32)


def _cross(x, g, wq, kv, wo, seq, tm=512):
    T = x.shape[0]
    tm = min(tm, seq)
    per_seq = seq // tm
    return pl.pallas_call(
        _cross_kernel,
        grid=(T // tm,),
        in_specs=[
            pl.BlockSpec((tm, D_MODEL), lambda i: (i, 0)),
            pl.BlockSpec((1, D_MODEL), lambda i: (0, 0)),
            pl.BlockSpec(wq.shape, lambda i: (0, 0)),
            pl.BlockSpec((MEM_LEN, 2 * WIDTH_X), lambda i: (i // per_seq, 0)),
            pl.BlockSpec(wo.shape, lambda i: (0, 0)),
        ],
        out_specs=pl.BlockSpec((tm, D_MODEL), lambda i: (i, 0)),
        out_shape=jax.ShapeDtypeStruct((T, D_MODEL), F32),
        compiler_params=_cparams(("parallel",)),
        name="cross",
    )(x, g, wq, kv, wo)


def _router_kernel(x_ref, g_ref, wr_ref, br_ref, cin_ref, hn_ref, ids_ref, wts_ref, rank_ref, cnt_ref, run_ref):
    i = pl.program_id(0)

    @pl.when(i == 0)
    def _():
        run_ref[...] = cin_ref[...]

    hn = _rms(x_ref[...], g_ref[...])
    hn_ref[...] = hn
    logits = jnp.dot(hn, wr_ref[...], preferred_element_type=F32, precision=lax.Precision.HIGHEST) + br_ref[...]
    tm = logits.shape[0]
    lane = lax.broadcasted_iota(jnp.int32, logits.shape, 1)
    lane_f = lane.astype(F32)
    vals = logits
    top_v, top_i, sels = [], [], []
    for _ in range(TOP_K):
        mk = jnp.max(vals, axis=-1, keepdims=True)
        idx = jnp.min(jnp.where(vals == mk, lane_f, float(LANES)), axis=-1, keepdims=True)
        sel = lane_f == idx
        top_v.append(mk)
        top_i.append(idx.astype(jnp.int32))
        sels.append(sel)
        vals = jnp.where(sel, 2.0 * NEG_BIG, vals)
    es = [jnp.exp(v - top_v[0]) for v in top_v]
    denom = es[0] + es[1] + es[2] + es[3]
    onehot = (sels[0] | sels[1] | sels[2] | sels[3]).astype(F32)
    r = lax.broadcasted_iota(jnp.int32, (tm, tm), 0)
    c = lax.broadcasted_iota(jnp.int32, (tm, tm), 1)
    tri = (c < r).astype(BF16)
    prefix = jnp.dot(tri, onehot.astype(BF16), preferred_element_type=F32)
    rank_te = prefix + run_ref[0:1, :]
    ids = jnp.zeros(logits.shape, jnp.int32)
    wts = jnp.zeros(logits.shape, F32)
    rank = jnp.zeros(logits.shape, jnp.int32)
    for k in range(TOP_K):
        rk = jnp.sum(jnp.where(sels[k], rank_te, 0.0), axis=-1, keepdims=True).astype(jnp.int32)
        ids = jnp.where(lane == k, top_i[k], ids)
        wts = jnp.where(lane == k, es[k] / denom, wts)
        rank = jnp.where(lane == k, rk, rank)
    ids_ref[...] = ids
    wts_ref[...] = wts
    rank_ref[...] = rank
    run_ref[...] = run_ref[...] + jnp.sum(onehot, axis=0, keepdims=True)
    cnt_ref[...] = run_ref[...]


def _router(x, g, wr, br, counts_in, tm=512):
    T = x.shape[0]
    row = pl.BlockSpec((tm, LANES), lambda i: (i, 0))
    return pl.pallas_call(
        _router_kernel,
        grid=(T // tm,),
        in_specs=[pl.BlockSpec((tm, D_MODEL), lambda i: (i, 0)), pl.BlockSpec((1, D_MODEL), lambda i: (0, 0)),
                  pl.BlockSpec((D_MODEL, LANES), lambda i: (0, 0)), pl.BlockSpec((1, LANES), lambda i: (0, 0)),
                  pl.BlockSpec((8, LANES), lambda i: (0, 0))],
        out_specs=[pl.BlockSpec((tm, D_MODEL), lambda i: (i, 0)), row, row, row,
                   pl.BlockSpec((8, LANES), lambda i: (0, 0))],
        out_shape=[jax.ShapeDtypeStruct((T, D_MODEL), F32), jax.ShapeDtypeStruct((T, LANES), jnp.int32),
                   jax.ShapeDtypeStruct((T, LANES), F32), jax.ShapeDtypeStruct((T, LANES), jnp.int32),
                   jax.ShapeDtypeStruct((8, LANES), F32)],
        scratch_shapes=[pltpu.VMEM((8, LANES), F32)],
        compiler_params=_cparams(("arbitrary",)),
        name="router",
    )(x, g, wr, br, counts_in)


DISPATCH_CHUNK = 512


ISSUE_UNROLL = 2


def _dispatch_kernel(slot_ref, hn_ref, xs_in_ref, xs_ref, sem):
    del xs_in_ref

    def issue(j, carry):
        for u in range(ISSUE_UNROLL):
            c = j * ISSUE_UNROLL + u
            for k in range(TOP_K):
                slot = slot_ref[0, 0, c * TOP_K + k]
                pltpu.make_async_copy(hn_ref.at[pl.ds(c, 1)], xs_ref.at[pl.ds(slot, 1)], sem).start()
        return carry

    lax.fori_loop(0, DISPATCH_CHUNK // ISSUE_UNROLL, issue, 0)
    for k in range(TOP_K):
        pltpu.make_async_copy(hn_ref, xs_ref.at[pl.ds(0, DISPATCH_CHUNK)], sem).wait()


def _zero_tail_kernel(tail_ref, o_ref):
    del tail_ref
    o_ref[...] = jnp.zeros(o_ref.shape, F32)


def _slot_buffer(tail_tile, n_slots):
    return pl.pallas_call(
        _zero_tail_kernel,
        grid_spec=pltpu.PrefetchScalarGridSpec(
            num_scalar_prefetch=1,
            grid=(tail_tile.shape[0],),
            in_specs=[],
            out_specs=pl.BlockSpec((MOE_TM, D_MODEL), lambda e, tail: (tail[e], 0)),
        ),
        out_shape=jax.ShapeDtypeStruct((n_slots, D_MODEL), F32),
        compiler_params=_cparams(("arbitrary",)),
        name="zero_tail",
    )(tail_tile)


def _dispatch(hn, slot4, xs0):
    T = hn.shape[0]
    n = T // DISPATCH_CHUNK
    n_slots = xs0.shape[0]
    idx_spec = pl.BlockSpec((1, 1, DISPATCH_CHUNK * TOP_K), lambda i: (i, 0, 0), memory_space=pltpu.SMEM)
    return pl.pallas_call(
        _dispatch_kernel,
        grid=(n,),
        in_specs=[idx_spec, pl.BlockSpec((DISPATCH_CHUNK, D_MODEL), lambda i: (i, 0)),
                  pl.BlockSpec(memory_space=pl.ANY)],
        out_specs=pl.BlockSpec(memory_space=pl.ANY),
        scratch_shapes=[pltpu.SemaphoreType.DMA(())],
        out_shape=jax.ShapeDtypeStruct((n_slots, D_MODEL), F32),
        input_output_aliases={2: 0},
        compiler_params=_cparams(("arbitrary",)),
        name="dispatch",
    )(slot4.reshape(n, 1, -1), hn, xs0)


SPLIT_TN = 512


def _split_glu_kernel(w_ref, perm_ref, g_ref, l_ref):
    r = jnp.dot(w_ref[0].astype(BF16), perm_ref[...], preferred_element_type=F32)
    half = SPLIT_TN // 2
    g_ref[0] = r[:, :half].astype(BF16)
    l_ref[0] = r[:, half:].astype(BF16)


def _split_glu(w1):
    E, K, N2 = w1.shape
    half = SPLIT_TN // 2
    src = jnp.arange(SPLIT_TN, dtype=jnp.int32)
    dst = jnp.where(src % 2 == 0, src // 2, half + src // 2)
    perm = (dst[:, None] == jnp.arange(SPLIT_TN, dtype=jnp.int32)[None, :]).astype(BF16)
    out = jax.ShapeDtypeStruct((E, K, N2 // 2), BF16)
    return pl.pallas_call(
        _split_glu_kernel,
        grid=(E, N2 // SPLIT_TN),
        in_specs=[pl.BlockSpec((1, K, SPLIT_TN), lambda e, j: (e, 0, j)),
                  pl.BlockSpec((SPLIT_TN, SPLIT_TN), lambda e, j: (0, 0))],
        out_specs=[pl.BlockSpec((1, K, half), lambda e, j: (e, 0, j)),
                   pl.BlockSpec((1, K, half), lambda e, j: (e, 0, j))],
        out_shape=[out, out],
        compiler_params=_cparams(("parallel", "parallel")),
        name="split_glu",
    )(w1, perm)


MOE_TM = 512
MOE_TF = 1024
MOE_SUB = 512


MOE_GROUP = 1


def _experts_kernel(ge_ref, tv_ref, xs_ref, w1g_ref, w1l_ref, b1g_ref, b1l_ref, w2_ref, b2_ref, ys_ref, xb_ref, acc_ref):
    i = pl.program_id(0)
    f = pl.program_id(1)
    s = pl.program_id(2)
    last = pl.num_programs(1) - 1
    valid = tv_ref[i * MOE_GROUP + s] > 0

    @pl.when(valid)
    def _():
        @pl.when(f == 0)
        def _():
            xb_ref[s] = xs_ref[...].astype(BF16)

        xb = xb_ref[s]
        part = None
        for c in range(MOE_TF // MOE_SUB):
            cols = slice(c * MOE_SUB, (c + 1) * MOE_SUB)
            glu = jnp.dot(xb, w1g_ref[0, :, cols], preferred_element_type=F32) + b1g_ref[0, :, cols]
            lin = jnp.dot(xb, w1l_ref[0, :, cols], preferred_element_type=F32) + b1l_ref[0, :, cols]
            glu = jnp.minimum(glu, SWIGLU_LIMIT)
            lin = jnp.clip(lin, -SWIGLU_LIMIT, SWIGLU_LIMIT)
            act = glu * jax.nn.sigmoid(SWIGLU_ALPHA * glu) * (lin + 1.0)
            sub_part = jnp.dot(act.astype(BF16), w2_ref[0, cols, :], preferred_element_type=F32)
            part = sub_part if part is None else part + sub_part

        @pl.when(f == 0)
        def _():
            acc_ref[s] = part

        @pl.when((f > 0) & (f < last))
        def _():
            acc_ref[s] += part

        @pl.when(f == last)
        def _():
            ys_ref[...] = acc_ref[s] + part + b2_ref[0]

    @pl.when(jnp.logical_not(valid) & (f == last))
    def _():
        ys_ref[...] = jnp.zeros(ys_ref.shape, F32)


def _experts(xs, group_expert, tile_valid, w1g, w1l, b1g, b1l, w2, b2):
    n_slots = xs.shape[0]
    n_groups = n_slots // (MOE_TM * MOE_GROUP)
    nf = D_FF // MOE_TF
    assert nf >= 2
    G = MOE_GROUP

    def xs_map(i, f, s, ge, tv):
        return (jnp.where(f == 0, i * G + s, i * G + G - 1), 0)

    def ys_map(i, f, s, ge, tv):
        return (jnp.where(f == nf - 1, i * G + s, i * G), 0)

    return pl.pallas_call(
        _experts_kernel,
        grid_spec=pltpu.PrefetchScalarGridSpec(
            num_scalar_prefetch=2,
            grid=(n_groups, nf, G),
            in_specs=[
                pl.BlockSpec((MOE_TM, D_MODEL), xs_map),
                pl.BlockSpec((1, D_MODEL, MOE_TF), lambda i, f, s, ge, tv: (ge[i], 0, f)),
                pl.BlockSpec((1, D_MODEL, MOE_TF), lambda i, f, s, ge, tv: (ge[i], 0, f)),
                pl.BlockSpec((1, 1, MOE_TF), lambda i, f, s, ge, tv: (ge[i], 0, f)),
                pl.BlockSpec((1, 1, MOE_TF), lambda i, f, s, ge, tv: (ge[i], 0, f)),
                pl.BlockSpec((1, MOE_TF, D_MODEL), lambda i, f, s, ge, tv: (ge[i], f, 0)),
                pl.BlockSpec((1, 1, D_MODEL), lambda i, f, s, ge, tv: (ge[i], 0, 0)),
            ],
            out_specs=pl.BlockSpec((MOE_TM, D_MODEL), ys_map),
            scratch_shapes=[pltpu.VMEM((G, MOE_TM, D_MODEL), BF16), pltpu.VMEM((G, MOE_TM, D_MODEL), F32)],
        ),
        out_shape=jax.ShapeDtypeStruct((n_slots, D_MODEL), F32),
        compiler_params=_cparams(("parallel", "arbitrary", "arbitrary")),
        name="experts",
    )(group_expert, tile_valid, xs, w1g, w1l, b1g, b1l, w2, b2)


COMBINE_CHUNK = 128


def _combine_kernel(slot_ref, slot_nxt_ref, x_ref, wts_ref, g_ref, ys_ref, o_ref, buf_ref, sem):
    i = pl.program_id(0)
    n = pl.num_programs(0)
    cur = i % 2

    def gather(slots, buf):
        def issue(j, carry):
            for u in range(ISSUE_UNROLL):
                c = j * ISSUE_UNROLL + u
                for k in range(TOP_K):
                    slot = slots[0, 0, c * TOP_K + k]
                    pltpu.make_async_copy(ys_ref.at[pl.ds(slot, 1)], buf_ref.at[buf, k, pl.ds(c, 1)],
                                          sem.at[buf]).start()
            return carry
        lax.fori_loop(0, COMBINE_CHUNK // ISSUE_UNROLL, issue, 0)

    @pl.when(i == 0)
    def _():
        gather(slot_ref, 0)

    @pl.when(i + 1 < n)
    def _():
        gather(slot_nxt_ref, 1 - cur)

    for k in range(TOP_K):
        pltpu.make_async_copy(ys_ref.at[pl.ds(0, COMBINE_CHUNK)], buf_ref.at[cur, k], sem.at[cur]).wait()
    wts = wts_ref[...]
    y = x_ref[...]
    for k in range(TOP_K):
        y = y + wts[:, k:k + 1] * buf_ref[cur, k]
    o_ref[...] = _rms(y, g_ref[...])


def _combine(x, wts, g, ys, slot4):
    T = x.shape[0]
    n = T // COMBINE_CHUNK
    idx_shape = (1, 1, COMBINE_CHUNK * TOP_K)
    idx_spec = pl.BlockSpec(idx_shape, lambda i: (i, 0, 0), memory_space=pltpu.SMEM)
    nxt_spec = pl.BlockSpec(idx_shape, lambda i: (jnp.minimum(i + 1, n - 1), 0, 0), memory_space=pltpu.SMEM)
    slots = slot4.reshape(n, 1, -1)
    return pl.pallas_call(
        _combine_kernel,
        grid=(n,),
        in_specs=[idx_spec, nxt_spec,
                  pl.BlockSpec((COMBINE_CHUNK, D_MODEL), lambda i: (i, 0)),
                  pl.BlockSpec((COMBINE_CHUNK, LANES), lambda i: (i, 0)),
                  pl.BlockSpec((1, D_MODEL), lambda i: (0, 0)),
                  pl.BlockSpec(memory_space=pl.ANY)],
        out_specs=pl.BlockSpec((COMBINE_CHUNK, D_MODEL), lambda i: (i, 0)),
        scratch_shapes=[pltpu.VMEM((2, TOP_K, COMBINE_CHUNK, D_MODEL), F32), pltpu.SemaphoreType.DMA((2,))],
        out_shape=jax.ShapeDtypeStruct((T, D_MODEL), F32),
        compiler_params=_cparams(("arbitrary",)),
        name="combine",
    )(slots, slots, x, wts, g, ys)


def _mix_and_cross(x3, mem3, w, lam_init):
    batch, seq, _ = x3.shape
    x = x3.reshape(batch * seq, D_MODEL)
    mem = mem3.reshape(batch * MEM_LEN, D_MODEL)
    p = _proj(x, w["norm_mix"], w["w_in"], _rope_tables(seq), w["qk_gain"], seq)
    oa = _attn_a(p, batch, seq)
    ob = _attn_b(p, w["lam_vecs"], w["subln_b"], batch, seq, lam_init)
    x = _merge(x, oa, ob, p, w["w_branch_a"], w["w_branch_b"], w["w_out"])
    kv = _norm_matmul(mem, w["norm_mem"], w["w_ckv"])
    return _cross(x, w["norm_cross"], w["w_cq"], kv, w["w_co"], seq)


def _moe_and_final_norm(xs_list, w):
    counts = jnp.zeros((8, LANES), F32)
    routed = []
    for x in xs_list:
        hn, ids, gate_w, rank, counts = _router(x, w["norm_moe"], w["w_router"], w["b_router"], counts)
        routed.append((hn, ids[:, :TOP_K], rank[:, :TOP_K], gate_w))
    rows_per_group = MOE_TM * MOE_GROUP
    cnt = counts[0, :N_EXPERTS].astype(jnp.int32)
    padded = ((cnt + rows_per_group - 1) // rows_per_group) * rows_per_group
    group_end = jnp.cumsum(padded)
    group_start = group_end - padded
    total = sum(x.shape[0] for x in xs_list) * TOP_K
    n_groups = total // rows_per_group + N_EXPERTS
    lo = jnp.arange(n_groups, dtype=jnp.int32) * rows_per_group
    group_expert = jnp.sum((lo[:, None] >= group_end[None, :]).astype(jnp.int32), axis=1)
    last_expert = jnp.max(jnp.where(cnt > 0, jnp.arange(N_EXPERTS, dtype=jnp.int32), 0))
    group_expert = jnp.minimum(group_expert, last_expert).astype(jnp.int32)
    tile_lo = jnp.arange(n_groups * MOE_GROUP, dtype=jnp.int32) * MOE_TM
    tile_e = jnp.repeat(group_expert, MOE_GROUP)
    tile_valid = (tile_lo < group_start[tile_e] + cnt[tile_e]).astype(jnp.int32)
    slot4s = [group_start[ids4] + rank4 for _, ids4, rank4, _ in routed]
    n_tiles = n_groups * MOE_GROUP
    assert MOE_GROUP == 1, "pad rows must be confined to each expert's last row tile"
    tail_tile = jnp.where(cnt > 0, group_end // MOE_TM - 1, n_tiles - 1).astype(jnp.int32)
    xs = _slot_buffer(tail_tile, n_tiles * MOE_TM)
    for (hn, _, _, _), slot4 in zip(routed, slot4s):
        xs = _dispatch(hn, slot4, xs)
    ys = _experts(xs, group_expert, tile_valid, w["w1g"], w["w1l"], w["b1g"], w["b1l"], w["w2"], w["b2"])
    return [_combine(x, gate_w, w["norm_final"], ys, slot4)
            for x, (_, _, _, gate_w), slot4 in zip(xs_list, routed, slot4s)]


def kernel(x_prompt, x_sample, mem_prompt, mem_sample, norm_mix, w_in, q_norm_a, k_norm_a, lambda_q1, lambda_k1, lambda_q2, lambda_k2, subln_b, w_branch_a, w_branch_b, w_out, norm_cross, norm_mem, w_cq, w_ckv, w_co, norm_moe, w_router, b_router, w_exp1, b_exp1, w_exp2, b_exp2, norm_final):
    assert DEPTH == 1
    wts = []
    for l in range(DEPTH):
        wr = jnp.zeros((D_MODEL, LANES), F32).at[:, :N_EXPERTS].set(w_router[l])
        br = jnp.full((1, LANES), NEG_BIG, F32).at[0, :N_EXPERTS].set(b_router[l])
        w1g, w1l = _split_glu(w_exp1[l])
        wts.append(dict(
            norm_mix=norm_mix[l][None, :],
            w_in=jnp.concatenate([w_in[l][:, REF_GATE_START:], w_in[l][:, :REF_GATE_START]], axis=1).astype(BF16),
            qk_gain=jnp.stack([q_norm_a[l], k_norm_a[l]]),
            lam_vecs=jnp.stack([lambda_q1[l], lambda_k1[l], lambda_q2[l], lambda_k2[l]]),
            subln_b=subln_b[l][None, :],
            w_branch_a=w_branch_a[l].astype(BF16),
            w_branch_b=w_branch_b[l].astype(BF16),
            w_out=w_out[l].astype(BF16),
            norm_cross=norm_cross[l][None, :],
            norm_mem=norm_mem[l][None, :],
            w_cq=w_cq[l].astype(BF16),
            w_ckv=w_ckv[l].astype(BF16),
            w_co=w_co[l].astype(BF16),
            norm_moe=norm_moe[l][None, :],
            w_router=wr,
            b_router=br,
            w1g=w1g,
            w1l=w1l,
            b1g=b_exp1[l][:, None, 0::2],
            b1l=b_exp1[l][:, None, 1::2],
            w2=w_exp2[l].astype(BF16),
            b2=b_exp2[l][:, None, :],
            norm_final=norm_final[None, :],
        ))
    w = wts[0]
    lam_init = 0.8 - 0.6 * math.exp(-0.3 * 0)
    streams = [_mix_and_cross(x_prompt, mem_prompt, w, lam_init), _mix_and_cross(x_sample, mem_sample, w, lam_init)]
    y_prompt, y_sample = _moe_and_final_norm(streams, w)
    return (y_prompt.reshape(x_prompt.shape), y_sample.reshape(x_sample.shape))
```

```python
import functools
import math

import jax
import jax.numpy as jnp
from jax import lax
from jax.experimental import pallas as pl
from jax.experimental.pallas import tpu as pltpu

F32 = jnp.float32
BF16 = jnp.bfloat16

D_MODEL = 2048
DEPTH = 1
GRID_W = 64
RMS_EPS = 1e-6
N_HEADS_A = 8
N_KV_A = 2
HEAD_DIM = 128
AXIAL_THETA = 10000.0
N_HEADS_B = 4
ROPE_THETA = 500000.0
ROT_DIM_B = HEAD_DIM // 4
MEM_LEN = 256
N_HEADS_X = 4
WIDTH_X = N_HEADS_X * HEAD_DIM
N_EXPERTS = 32
TOP_K = 4
D_FF = D_MODEL
SWIGLU_ALPHA = 1.702
SWIGLU_LIMIT = 7.0
LANES = 128
IN_COLS = 8704
QK_SCALE = HEAD_DIM ** -0.5
QK_SCALE_LOG2E = QK_SCALE * math.log2(math.e)
NEG_BIG = -1e30

VMEM_LIMIT = 56 * 1024 * 1024

COL_GA = 0
COL_GB = 2048
COL_QA = 4096
COL_KA = 5120
COL_VA = 5376
COL_QB = 5632
COL_KB = 6656
COL_VB = 7680
REF_GATE_START = 4608


def _cparams(sem):
    return pltpu.CompilerParams(dimension_semantics=sem, vmem_limit_bytes=VMEM_LIMIT)


def _rms(x, g):
    return x * lax.rsqrt(jnp.mean(x * x, axis=-1, keepdims=True) + RMS_EPS) * g


def _rotate(x, cos, sin_signed, shift):
    n = x.shape[-1]
    lane = lax.broadcasted_iota(jnp.int32, x.shape, 1)
    up = pltpu.roll(x, n - shift, 1)
    dn = pltpu.roll(x, shift, 1)
    partner = jnp.where((lane % (2 * shift)) < shift, up, dn)
    return x * cos + partner * sin_signed


PROJ_TN = 512


def _proj_kernel(x_ref, g_ref, w_ref, ca_ref, sa_ref, cb_ref, sb_ref, qk_ref, o_ref, xn_ref):
    j = pl.program_id(1)

    @pl.when(j == 0)
    def _():
        xn_ref[...] = _rms(x_ref[...], g_ref[...]).astype(BF16)

    tm = xn_ref.shape[0]
    n_sub = 2 if tm % 16 == 0 else 1
    sub = tm // n_sub

    def axial(gain, scale):
        def f(a, rows):
            r = _rotate(_rms(a, gain), ca_ref[rows, :], sa_ref[rows, :], HEAD_DIM // 4)
            return r * scale if scale != 1.0 else r
        return f

    def partial_rot(scale):
        def f(a, rows):
            r = _rotate(a, cb_ref[rows, :], sb_ref[rows, :], ROT_DIM_B // 2)
            return r * scale if scale != 1.0 else r
        return f

    def plain(a, rows):
        return a

    def gate(a, rows):
        return jax.nn.sigmoid(a)

    def run(epilogues):
        for r in range(n_sub):
            rows = slice(r * sub, (r + 1) * sub)
            acc = jnp.dot(xn_ref[rows, :], w_ref[...], preferred_element_type=F32)
            for c, fn in enumerate(epilogues):
                cols = slice(c * LANES, (c + 1) * LANES)
                o_ref[rows, cols] = fn(acc[:, cols], rows).astype(o_ref.dtype)

    n_chunks = PROJ_TN // LANES
    q_gain, k_gain = qk_ref[0:1, :], qk_ref[1:2, :]

    @pl.when(j < 8)
    def _():
        run([gate] * n_chunks)

    @pl.when((j >= 8) & (j < 10))
    def _():
        run([axial(q_gain, QK_SCALE_LOG2E)] * n_chunks)

    @pl.when(j == 10)
    def _():
        run([axial(k_gain, 1.0), axial(k_gain, 1.0), plain, plain])

    @pl.when((j >= 11) & (j < 13))
    def _():
        run([partial_rot(QK_SCALE_LOG2E)] * n_chunks)

    @pl.when((j >= 13) & (j < 15))
    def _():
        run([partial_rot(1.0)] * n_chunks)

    @pl.when(j >= 15)
    def _():
        run([plain] * n_chunks)


def _rope_tables(seq):
    t = jnp.arange(seq, dtype=jnp.int32)
    row = (t // GRID_W).astype(F32)
    col = (t % GRID_W).astype(F32)
    lane = jnp.arange(LANES, dtype=jnp.int32)
    half = HEAD_DIM // 4
    inv_a = jnp.power(AXIAL_THETA, -2.0 * (lane % half).astype(F32) / (HEAD_DIM // 2))
    pos_a = jnp.where(lane[None, :] < HEAD_DIM // 2, row[:, None], col[:, None])
    ang_a = pos_a * inv_a[None, :]
    sign_a = jnp.where((lane % (2 * half)) < half, -1.0, 1.0).astype(F32)
    ca, sa = jnp.cos(ang_a), jnp.sin(ang_a) * sign_a[None, :]
    hb = ROT_DIM_B // 2
    inv_b = jnp.power(ROPE_THETA, -2.0 * (lane % hb).astype(F32) / ROT_DIM_B)
    ang_b = t.astype(F32)[:, None] * inv_b[None, :]
    rot = (lane < ROT_DIM_B)[None, :]
    sign_b = jnp.where((lane % (2 * hb)) < hb, -1.0, 1.0).astype(F32)
    cb = jnp.where(rot, jnp.cos(ang_b), 1.0)
    sb = jnp.where(rot, jnp.sin(ang_b) * sign_b[None, :], 0.0)
    return ca, sa, cb, sb


def _proj(x, g, w, tables, qk_gain, seq, tm=1024):
    T = x.shape[0]
    tm = min(tm, seq)
    nseq = seq // tm
    tab_spec = pl.BlockSpec((tm, LANES), lambda i, j: (i % nseq, 0))
    return pl.pallas_call(
        _proj_kernel,
        grid=(T // tm, IN_COLS // PROJ_TN),
        in_specs=[
            pl.BlockSpec((tm, D_MODEL), lambda i, j: (i, 0)),
            pl.BlockSpec((1, D_MODEL), lambda i, j: (0, 0)),
            pl.BlockSpec((D_MODEL, PROJ_TN), lambda i, j: (0, j)),
            tab_spec, tab_spec, tab_spec, tab_spec,
            pl.BlockSpec((2, LANES), lambda i, j: (0, 0)),
        ],
        out_specs=pl.BlockSpec((tm, PROJ_TN), lambda i, j: (i, j)),
        out_shape=jax.ShapeDtypeStruct((T, IN_COLS), BF16),
        scratch_shapes=[pltpu.VMEM((tm, D_MODEL), BF16)],
        compiler_params=_cparams(("parallel", "arbitrary")),
        name="proj",
    )(x, g, w, *tables, qk_gain)


def _nt_dot(a, b):
    return lax.dot_general(a, b, (((1,), (1,)), ((), ())), preferred_element_type=F32)


def _lane_tile(x, n):
    return jnp.concatenate([x] * n, axis=1)


def _softmax_block(s, m_ref):
    m_prev = m_ref[...]
    m_new = jnp.maximum(m_prev, jnp.max(s, axis=-1, keepdims=True))
    alpha = jnp.exp2(m_prev - m_new)
    p = jnp.exp2(s - _lane_tile(m_new, s.shape[1] // LANES))
    m_ref[...] = m_new
    return alpha, p


def _attn_a_kernel(q_ref, k_ref, v_ref, o_ref, m_ref, acc_ref, vaug_ref, *, tk, unroll):
    tq = q_ref.shape[0]
    group = N_HEADS_A // N_KV_A

    @pl.when(pl.program_id(2) == 0)
    def _():
        vaug_ref[:, :HEAD_DIM] = v_ref[...]
        vaug_ref[:, HEAD_DIM:] = jnp.ones(v_ref.shape, BF16)

    m_ref[...] = jnp.full(m_ref.shape, NEG_BIG, F32)
    acc_ref[...] = jnp.zeros(acc_ref.shape, F32)

    def body(i, carry):
        for u in range(unroll):
            off = pl.multiple_of((i * unroll + u) * tk, tk)
            k = k_ref[pl.ds(off, tk), :]
            v = vaug_ref[pl.ds(off, tk), :]
            for g in range(group):
                rows = pl.ds(g * tq, tq)
                s = _nt_dot(q_ref[:, g * HEAD_DIM:(g + 1) * HEAD_DIM], k)
                alpha, p = _softmax_block(s, m_ref.at[rows])
                pv = jnp.dot(p.astype(BF16), v, preferred_element_type=F32)
                acc_ref[rows, :] = _lane_tile(alpha, 2) * acc_ref[rows, :] + pv
        return carry

    lax.fori_loop(0, k_ref.shape[0] // (tk * unroll), body, 0)
    o = acc_ref[:, :HEAD_DIM] / acc_ref[:, HEAD_DIM:]
    for g in range(group):
        o_ref[:, g * HEAD_DIM:(g + 1) * HEAD_DIM] = o[g * tq:(g + 1) * tq].astype(o_ref.dtype)


def _single_buffered(shape, index_map):
    return pl.BlockSpec(shape, index_map, pipeline_mode=pl.Buffered(1))


def _attn_a(p, batch, seq, tq=512, tk=2048):
    T = p.shape[0]
    group = N_HEADS_A // N_KV_A
    qw = group * HEAD_DIM
    tq = min(tq, seq)
    tk = min(tk, seq)
    nq = seq // tq
    unroll = max(u for u in (1, 2, 4) if (seq // tk) % u == 0)
    return pl.pallas_call(
        functools.partial(_attn_a_kernel, tk=tk, unroll=unroll),
        grid=(batch, N_KV_A, nq),
        in_specs=[
            pl.BlockSpec((tq, qw), lambda b, h, i: (b * nq + i, COL_QA // qw + h)),
            _single_buffered((seq, HEAD_DIM), lambda b, h, i: (b, COL_KA // HEAD_DIM + h)),
            _single_buffered((seq, HEAD_DIM), lambda b, h, i: (b, COL_VA // HEAD_DIM + h)),
        ],
        out_specs=pl.BlockSpec((tq, qw), lambda b, h, i: (b * nq + i, h)),
        out_shape=jax.ShapeDtypeStruct((T, N_HEADS_A * HEAD_DIM), BF16),
        scratch_shapes=[pltpu.VMEM((group * tq, LANES), F32),
                        pltpu.VMEM((group * tq, 2 * HEAD_DIM), F32),
                        pltpu.VMEM((seq, 2 * HEAD_DIM), BF16)],
        compiler_params=_cparams(("parallel", "parallel", "arbitrary")),
        name="attn_a",
    )(p, p, p)


def _attn_b_kernel(lam_ref, q_ref, k_ref, v_ref, sub_ref, o_ref, m_ref, l_ref, acc_ref, *, tk, sub, unroll, lam_init):
    tq = q_ref.shape[0]
    m_ref[...] = jnp.full(m_ref.shape, NEG_BIG, F32)
    l_ref[...] = jnp.zeros(l_ref.shape, F32)
    acc_ref[...] = jnp.zeros(acc_ref.shape, F32)

    def body(i, carry):
        for u in range(unroll):
            off = pl.multiple_of((i * unroll + u) * tk, tk)
            v = v_ref[pl.ds(off, tk), :]
            for c in range(2):
                cols = slice(c * HEAD_DIM, (c + 1) * HEAD_DIM)
                k = k_ref[pl.ds(off, tk), cols]
                for r in range(tq // sub):
                    rows = pl.ds(c * tq + r * sub, sub)
                    s = _nt_dot(q_ref[r * sub:(r + 1) * sub, cols], k)
                    alpha, p = _softmax_block(s, m_ref.at[rows])
                    l_ref[rows, :] = alpha * l_ref[rows, :] + jnp.sum(p, axis=-1, keepdims=True)
                    pv = jnp.dot(p.astype(BF16), v, preferred_element_type=F32)
                    acc_ref[rows, :] = _lane_tile(alpha, 2) * acc_ref[rows, :] + pv
        return carry

    lax.fori_loop(0, k_ref.shape[0] // (tk * unroll), body, 0)
    lv = lam_ref[...]
    lam = (jnp.exp(jnp.sum(lv[0:1] * lv[1:2], axis=-1, keepdims=True))
           - jnp.exp(jnp.sum(lv[2:3] * lv[3:4], axis=-1, keepdims=True)) + lam_init)
    o = acc_ref[...] / _lane_tile(l_ref[...], 2)
    od = o[:tq] - lam * o[tq:]
    o_ref[...] = (_rms(od, sub_ref[...]) * (1.0 - lam_init)).astype(o_ref.dtype)


def _attn_b(p, lam_vecs, subln, batch, seq, lam_init, tq=1024, sub=256, tk=2048):
    T = p.shape[0]
    vw = 2 * HEAD_DIM
    tq = min(tq, seq)
    sub = min(sub, tq)
    tk = min(tk, seq)
    nq = seq // tq
    unroll = 2 if (seq // tk) % 2 == 0 else 1
    return pl.pallas_call(
        functools.partial(_attn_b_kernel, tk=tk, sub=sub, unroll=unroll, lam_init=lam_init),
        grid=(batch, N_HEADS_B, nq),
        in_specs=[
            pl.BlockSpec((4, HEAD_DIM), lambda b, h, i: (0, 0)),
            pl.BlockSpec((tq, vw), lambda b, h, i: (b * nq + i, COL_QB // vw + h)),
            _single_buffered((seq, vw), lambda b, h, i: (b, COL_KB // vw + h)),
            _single_buffered((seq, vw), lambda b, h, i: (b, COL_VB // vw + h)),
            pl.BlockSpec((1, vw), lambda b, h, i: (0, 0)),
        ],
        out_specs=pl.BlockSpec((tq, vw), lambda b, h, i: (b * nq + i, h)),
        out_shape=jax.ShapeDtypeStruct((T, N_HEADS_B * vw), BF16),
        scratch_shapes=[pltpu.VMEM((2 * tq, LANES), F32), pltpu.VMEM((2 * tq, LANES), F32),
                        pltpu.VMEM((2 * tq, vw), F32)],
        compiler_params=_cparams(("parallel", "parallel", "parallel")),
        name="attn_b",
    )(lam_vecs, p, p, p, subln)


def _merge_kernel(x_ref, oa_ref, ob_ref, ga_ref, gb_ref, wa_ref, wb_ref, wo_ref, o_ref):
    a = jnp.dot(oa_ref[...], wa_ref[...], preferred_element_type=F32)
    b = jnp.dot(ob_ref[...], wb_ref[...], preferred_element_type=F32)
    merged = ga_ref[...].astype(F32) * a + gb_ref[...].astype(F32) * b
    o_ref[...] = x_ref[...] + jnp.dot(merged.astype(BF16), wo_ref[...], preferred_element_type=F32)


def _resident(shape):
    return pl.BlockSpec(shape, lambda *_: (0,) * len(shape), pipeline_mode=pl.Buffered(1))


def _merge(x, oa, ob, p, wa, wb, wo, tm=256):
    T = x.shape[0]
    return pl.pallas_call(
        _merge_kernel,
        grid=(T // tm,),
        in_specs=[
            pl.BlockSpec((tm, D_MODEL), lambda i: (i, 0)),
            pl.BlockSpec((tm, oa.shape[1]), lambda i: (i, 0)),
            pl.BlockSpec((tm, ob.shape[1]), lambda i: (i, 0)),
            pl.BlockSpec((tm, D_MODEL), lambda i: (i, COL_GA // D_MODEL)),
            pl.BlockSpec((tm, D_MODEL), lambda i: (i, COL_GB // D_MODEL)),
            _resident(wa.shape), _resident(wb.shape), _resident(wo.shape),
        ],
        out_specs=pl.BlockSpec((tm, D_MODEL), lambda i: (i, 0)),
        out_shape=jax.ShapeDtypeStruct((T, D_MODEL), F32),
        compiler_params=_cparams(("parallel",)),
        name="merge",
    )(x, oa, ob, p, p, wa, wb, wo)


def _norm_matmul_kernel(x_ref, g_ref, w_ref, o_ref):
    xn = _rms(x_ref[...], g_ref[...]).astype(BF16)
    o_ref[...] = jnp.dot(xn, w_ref[...], preferred_element_type=F32).astype(o_ref.dtype)


def _norm_matmul(x, g, w, tm=256):
    T, K = x.shape
    N = w.shape[1]
    return pl.pallas_call(
        _norm_matmul_kernel,
        grid=(T // tm,),
        in_specs=[pl.BlockSpec((tm, K), lambda i: (i, 0)), pl.BlockSpec((1, K), lambda i: (0, 0)),
                  pl.BlockSpec((K, N), lambda i: (0, 0))],
        out_specs=pl.BlockSpec((tm, N), lambda i: (i, 0)),
        out_shape=jax.ShapeDtypeStruct((T, N), BF16),
        compiler_params=_cparams(("parallel",)),
        name="mem_kv",
    )(x, g, w)


def _cross_kernel(x_ref, g_ref, wq_ref, kv_ref, wo_ref, o_ref):
    x = x_ref[...]
    hn = _rms(x, g_ref[...]).astype(BF16)
    q = (jnp.dot(hn, wq_ref[...], preferred_element_type=F32) * QK_SCALE).astype(BF16)
    heads = []
    for h in range(N_HEADS_X):
        lo = h * HEAD_DIM
        s = _nt_dot(q[:, lo:lo + HEAD_DIM], kv_ref[:, lo:lo + HEAD_DIM])
        e = jnp.exp(s - jnp.max(s, axis=-1, keepdims=True))
        pr = (e / jnp.sum(e, axis=-1, keepdims=True)).astype(BF16)
        heads.append(jnp.dot(pr, kv_ref[:, WIDTH_X + lo:WIDTH_X + lo + HEAD_DIM], preferred_element_type=F32))
    o = jnp.concatenate(heads, axis=1).astype(BF16)
    o_ref[...] = x + jnp.dot(o, wo_ref[...], preferred_element_type=F32)


def _cross(x, g, wq, kv, wo, seq, tm=512):
    T = x.shape[0]
    tm = min(tm, seq)
    per_seq = seq // tm
    return pl.pallas_call(
        _cross_kernel,
        grid=(T // tm,),
        in_specs=[
            pl.BlockSpec((tm, D_MODEL), lambda i: (i, 0)),
            pl.BlockSpec((1, D_MODEL), lambda i: (0, 0)),
            pl.BlockSpec(wq.shape, lambda i: (0, 0)),
            pl.BlockSpec((MEM_LEN, 2 * WIDTH_X), lambda i: (i // per_seq, 0)),
            pl.BlockSpec(wo.shape, lambda i: (0, 0)),
        ],
        out_specs=pl.BlockSpec((tm, D_MODEL), lambda i: (i, 0)),
        out_shape=jax.ShapeDtypeStruct((T, D_MODEL), F32),
        compiler_params=_cparams(("parallel",)),
        name="cross",
    )(x, g, wq, kv, wo)


def _router_kernel(x_ref, g_ref, wr_ref, br_ref, cin_ref, hn_ref, ids_ref, wts_ref, rank_ref, cnt_ref, run_ref):
    i = pl.program_id(0)

    @pl.when(i == 0)
    def _():
        run_ref[...] = cin_ref[...]

    hn = _rms(x_ref[...], g_ref[...])
    hn_ref[...] = hn
    logits = jnp.dot(hn, wr_ref[...], preferred_element_type=F32, precision=lax.Precision.HIGHEST) + br_ref[...]
    tm = logits.shape[0]
    lane = lax.broadcasted_iota(jnp.int32, logits.shape, 1)
    lane_f = lane.astype(F32)
    vals = logits
    top_v, top_i, sels = [], [], []
    for _ in range(TOP_K):
        mk = jnp.max(vals, axis=-1, keepdims=True)
        idx = jnp.min(jnp.where(vals == mk, lane_f, float(LANES)), axis=-1, keepdims=True)
        sel = lane_f == idx
        top_v.append(mk)
        top_i.append(idx.astype(jnp.int32))
        sels.append(sel)
        vals = jnp.where(sel, 2.0 * NEG_BIG, vals)
    es = [jnp.exp(v - top_v[0]) for v in top_v]
    denom = es[0] + es[1] + es[2] + es[3]
    onehot = (sels[0] | sels[1] | sels[2] | sels[3]).astype(F32)
    r = lax.broadcasted_iota(jnp.int32, (tm, tm), 0)
    c = lax.broadcasted_iota(jnp.int32, (tm, tm), 1)
    tri = (c < r).astype(BF16)
    prefix = jnp.dot(tri, onehot.astype(BF16), preferred_element_type=F32)
    rank_te = prefix + run_ref[0:1, :]
    ids = jnp.zeros(logits.shape, jnp.int32)
    wts = jnp.zeros(logits.shape, F32)
    rank = jnp.zeros(logits.shape, jnp.int32)
    for k in range(TOP_K):
        rk = jnp.sum(jnp.where(sels[k], rank_te, 0.0), axis=-1, keepdims=True).astype(jnp.int32)
        ids = jnp.where(lane == k, top_i[k], ids)
        wts = jnp.where(lane == k, es[k] / denom, wts)
        rank = jnp.where(lane == k, rk, rank)
    ids_ref[...] = ids
    wts_ref[...] = wts
    rank_ref[...] = rank
    run_ref[...] = run_ref[...] + jnp.sum(onehot, axis=0, keepdims=True)
    cnt_ref[...] = run_ref[...]


def _router(x, g, wr, br, counts_in, tm=512):
    T = x.shape[0]
    row = pl.BlockSpec((tm, LANES), lambda i: (i, 0))
    return pl.pallas_call(
        _router_kernel,
        grid=(T // tm,),
        in_specs=[pl.BlockSpec((tm, D_MODEL), lambda i: (i, 0)), pl.BlockSpec((1, D_MODEL), lambda i: (0, 0)),
                  pl.BlockSpec((D_MODEL, LANES), lambda i: (0, 0)), pl.BlockSpec((1, LANES), lambda i: (0, 0)),
                  pl.BlockSpec((8, LANES), lambda i: (0, 0))],
        out_specs=[pl.BlockSpec((tm, D_MODEL), lambda i: (i, 0)), row, row, row,
                   pl.BlockSpec((8, LANES), lambda i: (0, 0))],
        out_shape=[jax.ShapeDtypeStruct((T, D_MODEL), F32), jax.ShapeDtypeStruct((T, LANES), jnp.int32),
                   jax.ShapeDtypeStruct((T, LANES), F32), jax.ShapeDtypeStruct((T, LANES), jnp.int32),
                   jax.ShapeDtypeStruct((8, LANES), F32)],
        scratch_shapes=[pltpu.VMEM((8, LANES), F32)],
        compiler_params=_cparams(("arbitrary",)),
        name="router",
    )(x, g, wr, br, counts_in)


DISPATCH_CHUNK = 512


ISSUE_UNROLL = 2


def _dispatch_kernel(slot_ref, hn_ref, xs_in_ref, xs_ref, sem):
    del xs_in_ref

    def issue(j, carry):
        for u in range(ISSUE_UNROLL):
            c = j * ISSUE_UNROLL + u
            for k in range(TOP_K):
                slot = slot_ref[0, 0, c * TOP_K + k]
                pltpu.make_async_copy(hn_ref.at[pl.ds(c, 1)], xs_ref.at[pl.ds(slot, 1)], sem).start(priority=k % 2)
        return carry

    lax.fori_loop(0, DISPATCH_CHUNK // ISSUE_UNROLL, issue, 0)
    for k in range(TOP_K):
        pltpu.make_async_copy(hn_ref, xs_ref.at[pl.ds(0, DISPATCH_CHUNK)], sem).wait()


def _zero_tail_kernel(tail_ref, o_ref):
    del tail_ref
    o_ref[...] = jnp.zeros(o_ref.shape, F32)


def _slot_buffer(tail_tile, n_slots):
    return pl.pallas_call(
        _zero_tail_kernel,
        grid_spec=pltpu.PrefetchScalarGridSpec(
            num_scalar_prefetch=1,
            grid=(tail_tile.shape[0],),
            in_specs=[],
            out_specs=pl.BlockSpec((MOE_TM, D_MODEL), lambda e, tail: (tail[e], 0)),
        ),
        out_shape=jax.ShapeDtypeStruct((n_slots, D_MODEL), F32),
        compiler_params=_cparams(("arbitrary",)),
        name="zero_tail",
    )(tail_tile)


def _dispatch(hn, slot4, xs0):
    T = hn.shape[0]
    n = T // DISPATCH_CHUNK
    n_slots = xs0.shape[0]
    idx_spec = pl.BlockSpec((1, 1, DISPATCH_CHUNK * TOP_K), lambda i: (i, 0, 0), memory_space=pltpu.SMEM)
    return pl.pallas_call(
        _dispatch_kernel,
        grid=(n,),
        in_specs=[idx_spec, pl.BlockSpec((DISPATCH_CHUNK, D_MODEL), lambda i: (i, 0)),
                  pl.BlockSpec(memory_space=pl.ANY)],
        out_specs=pl.BlockSpec(memory_space=pl.ANY),
        scratch_shapes=[pltpu.SemaphoreType.DMA(())],
        out_shape=jax.ShapeDtypeStruct((n_slots, D_MODEL), F32),
        input_output_aliases={2: 0},
        compiler_params=_cparams(("arbitrary",)),
        name="dispatch",
    )(slot4.reshape(n, 1, -1), hn, xs0)


SPLIT_TN = 512


def _split_glu_kernel(w_ref, perm_ref, g_ref, l_ref):
    r = jnp.dot(w_ref[0].astype(BF16), perm_ref[...], preferred_element_type=F32)
    half = SPLIT_TN // 2
    g_ref[0] = r[:, :half].astype(BF16)
    l_ref[0] = r[:, half:].astype(BF16)


def _split_glu(w1):
    E, K, N2 = w1.shape
    half = SPLIT_TN // 2
    src = jnp.arange(SPLIT_TN, dtype=jnp.int32)
    dst = jnp.where(src % 2 == 0, src // 2, half + src // 2)
    perm = (dst[:, None] == jnp.arange(SPLIT_TN, dtype=jnp.int32)[None, :]).astype(BF16)
    out = jax.ShapeDtypeStruct((E, K, N2 // 2), BF16)
    return pl.pallas_call(
        _split_glu_kernel,
        grid=(E, N2 // SPLIT_TN),
        in_specs=[pl.BlockSpec((1, K, SPLIT_TN), lambda e, j: (e, 0, j)),
                  pl.BlockSpec((SPLIT_TN, SPLIT_TN), lambda e, j: (0, 0))],
        out_specs=[pl.BlockSpec((1, K, half), lambda e, j: (e, 0, j)),
                   pl.BlockSpec((1, K, half), lambda e, j: (e, 0, j))],
        out_shape=[out, out],
        compiler_params=_cparams(("parallel", "parallel")),
        name="split_glu",
    )(w1, perm)


MOE_TM = 512
MOE_TF = 1024
MOE_SUB = 512


MOE_GROUP = 1


def _experts_kernel(ge_ref, tv_ref, xs_ref, w1g_ref, w1l_ref, b1g_ref, b1l_ref, w2_ref, b2_ref, ys_ref, xb_ref, acc_ref):
    i = pl.program_id(0)
    f = pl.program_id(1)
    s = pl.program_id(2)
    last = pl.num_programs(1) - 1
    valid = tv_ref[i * MOE_GROUP + s] > 0

    @pl.when(valid)
    def _():
        @pl.when(f == 0)
        def _():
            xb_ref[s] = xs_ref[...].astype(BF16)

        xb = xb_ref[s]
        part = None
        for c in range(MOE_TF // MOE_SUB):
            cols = slice(c * MOE_SUB, (c + 1) * MOE_SUB)
            glu = jnp.dot(xb, w1g_ref[0, :, cols], preferred_element_type=F32) + b1g_ref[0, :, cols]
            lin = jnp.dot(xb, w1l_ref[0, :, cols], preferred_element_type=F32) + b1l_ref[0, :, cols]
            glu = jnp.minimum(glu, SWIGLU_LIMIT)
            lin = jnp.clip(lin, -SWIGLU_LIMIT, SWIGLU_LIMIT)
            act = glu * jax.nn.sigmoid(SWIGLU_ALPHA * glu) * (lin + 1.0)
            sub_part = jnp.dot(act.astype(BF16), w2_ref[0, cols, :], preferred_element_type=F32)
            part = sub_part if part is None else part + sub_part

        @pl.when(f == 0)
        def _():
            acc_ref[s] = part

        @pl.when((f > 0) & (f < last))
        def _():
            acc_ref[s] += part

        @pl.when(f == last)
        def _():
            ys_ref[...] = acc_ref[s] + part + b2_ref[0]

    @pl.when(jnp.logical_not(valid) & (f == last))
    def _():
        ys_ref[...] = jnp.zeros(ys_ref.shape, F32)


def _experts(xs, group_expert, tile_valid, w1g, w1l, b1g, b1l, w2, b2):
    n_slots = xs.shape[0]
    n_groups = n_slots // (MOE_TM * MOE_GROUP)
    nf = D_FF // MOE_TF
    assert nf >= 2
    G = MOE_GROUP

    def xs_map(i, f, s, ge, tv):
        return (jnp.where(f == 0, i * G + s, i * G + G - 1), 0)

    def ys_map(i, f, s, ge, tv):
        return (jnp.where(f == nf - 1, i * G + s, i * G), 0)

    return pl.pallas_call(
        _experts_kernel,
        grid_spec=pltpu.PrefetchScalarGridSpec(
            num_scalar_prefetch=2,
            grid=(n_groups, nf, G),
            in_specs=[
                pl.BlockSpec((MOE_TM, D_MODEL), xs_map),
                pl.BlockSpec((1, D_MODEL, MOE_TF), lambda i, f, s, ge, tv: (ge[i], 0, f)),
                pl.BlockSpec((1, D_MODEL, MOE_TF), lambda i, f, s, ge, tv: (ge[i], 0, f)),
                pl.BlockSpec((1, 1, MOE_TF), lambda i, f, s, ge, tv: (ge[i], 0, f)),
                pl.BlockSpec((1, 1, MOE_TF), lambda i, f, s, ge, tv: (ge[i], 0, f)),
                pl.BlockSpec((1, MOE_TF, D_MODEL), lambda i, f, s, ge, tv: (ge[i], f, 0)),
                pl.BlockSpec((1, 1, D_MODEL), lambda i, f, s, ge, tv: (ge[i], 0, 0)),
            ],
            out_specs=pl.BlockSpec((MOE_TM, D_MODEL), ys_map),
            scratch_shapes=[pltpu.VMEM((G, MOE_TM, D_MODEL), BF16), pltpu.VMEM((G, MOE_TM, D_MODEL), F32)],
        ),
        out_shape=jax.ShapeDtypeStruct((n_slots, D_MODEL), F32),
        compiler_params=_cparams(("parallel", "arbitrary", "arbitrary")),
        name="experts",
    )(group_expert, tile_valid, xs, w1g, w1l, b1g, b1l, w2, b2)


COMBINE_CHUNK = 128


def _combine_kernel(slot_ref, slot_nxt_ref, x_ref, wts_ref, g_ref, ys_ref, o_ref, buf_ref, sem):
    i = pl.program_id(0)
    n = pl.num_programs(0)
    cur = i % 2

    def gather(slots, buf):
        def issue(j, carry):
            for u in range(ISSUE_UNROLL):
                c = j * ISSUE_UNROLL + u
                for k in range(TOP_K):
                    slot = slots[0, 0, c * TOP_K + k]
                    pltpu.make_async_copy(ys_ref.at[pl.ds(slot, 1)], buf_ref.at[buf, k, pl.ds(c, 1)],
                                          sem.at[buf]).start(priority=k % 2)
            return carry
        lax.fori_loop(0, COMBINE_CHUNK // ISSUE_UNROLL, issue, 0)

    @pl.when(i == 0)
    def _():
        gather(slot_ref, 0)

    @pl.when(i + 1 < n)
    def _():
        gather(slot_nxt_ref, 1 - cur)

    for k in range(TOP_K):
        pltpu.make_async_copy(ys_ref.at[pl.ds(0, COMBINE_CHUNK)], buf_ref.at[cur, k], sem.at[cur]).wait()
    wts = wts_ref[...]
    y = x_ref[...]
    for k in range(TOP_K):
        y = y + wts[:, k:k + 1] * buf_ref[cur, k]
    o_ref[...] = _rms(y, g_ref[...])


def _combine(x, wts, g, ys, slot4):
    T = x.shape[0]
    n = T // COMBINE_CHUNK
    idx_shape = (1, 1, COMBINE_CHUNK * TOP_K)
    idx_spec = pl.BlockSpec(idx_shape, lambda i: (i, 0, 0), memory_space=pltpu.SMEM)
    nxt_spec = pl.BlockSpec(idx_shape, lambda i: (jnp.minimum(i + 1, n - 1), 0, 0), memory_space=pltpu.SMEM)
    slots = slot4.reshape(n, 1, -1)
    return pl.pallas_call(
        _combine_kernel,
        grid=(n,),
        in_specs=[idx_spec, nxt_spec,
                  pl.BlockSpec((COMBINE_CHUNK, D_MODEL), lambda i: (i, 0)),
                  pl.BlockSpec((COMBINE_CHUNK, LANES), lambda i: (i, 0)),
                  pl.BlockSpec((1, D_MODEL), lambda i: (0, 0)),
                  pl.BlockSpec(memory_space=pl.ANY)],
        out_specs=pl.BlockSpec((COMBINE_CHUNK, D_MODEL), lambda i: (i, 0)),
        scratch_shapes=[pltpu.VMEM((2, TOP_K, COMBINE_CHUNK, D_MODEL), F32), pltpu.SemaphoreType.DMA((2,))],
        out_shape=jax.ShapeDtypeStruct((T, D_MODEL), F32),
        compiler_params=_cparams(("arbitrary",)),
        name="combine",
    )(slots, slots, x, wts, g, ys)


def _mix_and_cross(x3, mem3, w, lam_init):
    batch, seq, _ = x3.shape
    x = x3.reshape(batch * seq, D_MODEL)
    mem = mem3.reshape(batch * MEM_LEN, D_MODEL)
    p = _proj(x, w["norm_mix"], w["w_in"], _rope_tables(seq), w["qk_gain"], seq)
    oa = _attn_a(p, batch, seq)
    ob = _attn_b(p, w["lam_vecs"], w["subln_b"], batch, seq, lam_init)
    x = _merge(x, oa, ob, p, w["w_branch_a"], w["w_branch_b"], w["w_out"])
    kv = _norm_matmul(mem, w["norm_mem"], w["w_ckv"])
    return _cross(x, w["norm_cross"], w["w_cq"], kv, w["w_co"], seq)


def _moe_and_final_norm(xs_list, w):
    counts = jnp.zeros((8, LANES), F32)
    routed = []
    for x in xs_list:
        hn, ids, gate_w, rank, counts = _router(x, w["norm_moe"], w["w_router"], w["b_router"], counts)
        routed.append((hn, ids[:, :TOP_K], rank[:, :TOP_K], gate_w))
    rows_per_group = MOE_TM * MOE_GROUP
    cnt = counts[0, :N_EXPERTS].astype(jnp.int32)
    padded = ((cnt + rows_per_group - 1) // rows_per_group) * rows_per_group
    group_end = jnp.cumsum(padded)
    group_start = group_end - padded
    total = sum(x.shape[0] for x in xs_list) * TOP_K
    n_groups = total // rows_per_group + N_EXPERTS
    lo = jnp.arange(n_groups, dtype=jnp.int32) * rows_per_group
    group_expert = jnp.sum((lo[:, None] >= group_end[None, :]).astype(jnp.int32), axis=1)
    last_expert = jnp.max(jnp.where(cnt > 0, jnp.arange(N_EXPERTS, dtype=jnp.int32), 0))
    group_expert = jnp.minimum(group_expert, last_expert).astype(jnp.int32)
    tile_lo = jnp.arange(n_groups * MOE_GROUP, dtype=jnp.int32) * MOE_TM
    tile_e = jnp.repeat(group_expert, MOE_GROUP)
    tile_valid = (tile_lo < group_start[tile_e] + cnt[tile_e]).astype(jnp.int32)
    slot4s = [group_start[ids4] + rank4 for _, ids4, rank4, _ in routed]
    n_tiles = n_groups * MOE_GROUP
    assert MOE_GROUP == 1, "pad rows must be confined to each expert's last row tile"
    tail_tile = jnp.where(cnt > 0, group_end // MOE_TM - 1, n_tiles - 1).astype(jnp.int32)
    xs = _slot_buffer(tail_tile, n_tiles * MOE_TM)
    for (hn, _, _, _), slot4 in zip(routed, slot4s):
        xs = _dispatch(hn, slot4, xs)
    ys = _experts(xs, group_expert, tile_valid, w["w1g"], w["w1l"], w["b1g"], w["b1l"], w["w2"], w["b2"])
    return [_combine(x, gate_w, w["norm_final"], ys, slot4)
            for x, (_, _, _, gate_w), slot4 in zip(xs_list, routed, slot4s)]


def kernel(x_prompt, x_sample, mem_prompt, mem_sample, norm_mix, w_in, q_norm_a, k_norm_a, lambda_q1, lambda_k1, lambda_q2, lambda_k2, subln_b, w_branch_a, w_branch_b, w_out, norm_cross, norm_mem, w_cq, w_ckv, w_co, norm_moe, w_router, b_router, w_exp1, b_exp1, w_exp2, b_exp2, norm_final):
    assert DEPTH == 1
    wts = []
    for l in range(DEPTH):
        wr = jnp.zeros((D_MODEL, LANES), F32).at[:, :N_EXPERTS].set(w_router[l])
        br = jnp.full((1, LANES), NEG_BIG, F32).at[0, :N_EXPERTS].set(b_router[l])
        w1g, w1l = _split_glu(w_exp1[l])
        wts.append(dict(
            norm_mix=norm_mix[l][None, :],
            w_in=jnp.concatenate([w_in[l][:, REF_GATE_START:], w_in[l][:, :REF_GATE_START]], axis=1).astype(BF16),
            qk_gain=jnp.stack([q_norm_a[l], k_norm_a[l]]),
            lam_vecs=jnp.stack([lambda_q1[l], lambda_k1[l], lambda_q2[l], lambda_k2[l]]),
            subln_b=subln_b[l][None, :],
            w_branch_a=w_branch_a[l].astype(BF16),
            w_branch_b=w_branch_b[l].astype(BF16),
            w_out=w_out[l].astype(BF16),
            norm_cross=norm_cross[l][None, :],
            norm_mem=norm_mem[l][None, :],
            w_cq=w_cq[l].astype(BF16),
            w_ckv=w_ckv[l].astype(BF16),
            w_co=w_co[l].astype(BF16),
            norm_moe=norm_moe[l][None, :],
            w_router=wr,
            b_router=br,
            w1g=w1g,
            w1l=w1l,
            b1g=b_exp1[l][:, None, 0::2],
            b1l=b_exp1[l][:, None, 1::2],
            w2=w_exp2[l].astype(BF16),
            b2=b_exp2[l][:, None, :],
            norm_final=norm_final[None, :],
        ))
    w = wts[0]
    lam_init = 0.8 - 0.6 * math.exp(-0.3 * 0)
    streams = [_mix_and_cross(x_prompt, mem_prompt, w, lam_init), _mix_and_cross(x_sample, mem_sample, w, lam_init)]
    y_prompt, y_sample = _moe_and_final_norm(streams, w)
    return (y_prompt.reshape(x_prompt.shape), y_sample.reshape(x_sample.shape))
```
